```python
import math
import jax, jax.numpy as jnp
from jax import lax
import numpy as np

D_MODEL = 1024
BATCH = 16
SEQ = 2048
DEPTH = 2
DEC_BATCH = 128
DEC_SEQ = 8
PAST_LEN = 16384
PAGE_SIZE = 128

HEAD_DIM = 64
GROUP_WIDTH = D_MODEL // 4
MIX_WIDTH = 4 * GROUP_WIDTH
FOX_HEADS = GROUP_WIDTH // HEAD_DIM
FOX_KV_HEADS = FOX_HEADS // 2
FOX_GROUP = FOX_HEADS // FOX_KV_HEADS
FORGET_BIAS = 3.0
MLSTM_HEADS = GROUP_WIDTH // HEAD_DIM
MLSTM_W = MLSTM_HEADS * HEAD_DIM
MLSTM_CHUNK = 64
S5_CH = GROUP_WIDTH
S5_GROUP_SIZE = 16
S5_GROUPS = S5_CH // S5_GROUP_SIZE
S5_STATE = 64
MLA_HEADS = GROUP_WIDTH // HEAD_DIM
MLA_NOPE = HEAD_DIM
MLA_ROPE = HEAD_DIM // 2
MLA_V = HEAD_DIM
MLA_Q_RANK = D_MODEL // 8
MLA_KV_RANK = D_MODEL // 8
ROPE_THETA = 10000.0
PEER_HEADS = 8
PEER_NKEYS = 128
PEER_EXPERTS = PEER_NKEYS * PEER_NKEYS
PEER_KEY_DIM = 256
PEER_TOPK = 16
PEER_TOKEN_BLOCK = 256
PLE_DIM = 256
Q_BLOCK = 128
MAX_KEY_BLOCK_PAGES = 8
EPS = 1e-6
F32 = jnp.float32

SPLIT_SIZES = (FOX_HEADS * HEAD_DIM, FOX_KV_HEADS * HEAD_DIM, FOX_KV_HEADS * HEAD_DIM, FOX_HEADS,
               MLSTM_W, MLSTM_W, MLSTM_W, MLSTM_HEADS, MLSTM_HEADS, MLSTM_W,
               S5_CH,
               MLA_Q_RANK, MLA_KV_RANK, MLA_ROPE)
D_IN = sum(SPLIT_SIZES)

kernel_name = 'hymba_fox_mlstm_s5_mla_peer_step'


def rmsnorm(x, g):
    xf = x.astype(F32)
    y = xf * lax.rsqrt(jnp.mean(xf * xf, axis=-1, keepdims=True) + EPS)
    return (y * g.astype(F32)).astype(x.dtype)


def rope(x, pos):
    half = x.shape[-1] // 2
    inv_freq = ROPE_THETA ** (-jnp.arange(half, dtype=F32) / half)
    ang = pos.astype(F32)[:, None] * inv_freq[None, :]
    cos = jnp.cos(ang)[None, :, None, :]
    sin = jnp.sin(ang)[None, :, None, :]
    xf = x.astype(F32)
    x1, x2 = xf[..., :half], xf[..., half:]
    return jnp.concatenate([x1 * cos - x2 * sin, x2 * cos + x1 * sin], axis=-1).astype(x.dtype)


def prompt_attention(q, k, v, fcum):
    bsz, seq, n_kv, n_grp, dk = q.shape
    dv = v.shape[-1]
    qb = math.gcd(seq, Q_BLOCK)
    scale = dk ** -0.5
    kpos = jnp.arange(seq)
    ft = None if fcum is None else jnp.transpose(fcum, (0, 2, 3, 1))

    def one_block(blk):
        start = blk * qb
        qblk = lax.dynamic_slice_in_dim(q, start, qb, axis=1)
        s = jnp.einsum('bqhgd,bkhd->bhgqk', qblk, k).astype(F32) * scale
        if ft is not None:
            fq = lax.dynamic_slice_in_dim(ft, start, qb, axis=3)
            s = s + fq[..., :, None] - ft[..., None, :]
        qpos = start + jnp.arange(qb)
        s = jnp.where(qpos[:, None] >= kpos[None, :], s, -jnp.inf)
        p = jax.nn.softmax(s, axis=-1)
        return jnp.einsum('bhgqk,bkhe->bqhge', p.astype(v.dtype), v)

    out = lax.map(one_block, jnp.arange(seq // qb))
    return jnp.moveaxis(out, 0, 1).reshape(bsz, seq, n_kv, n_grp, dv)


def _online_update(m, l, acc, s, v):
    m_new = jnp.maximum(m, s.max(-1))
    p = jnp.exp(s - m_new[..., None])
    corr = jnp.exp(m - m_new)
    acc = acc * corr[..., None] + jnp.einsum('bhgts,bshe->bhgte', p, v.astype(F32))
    return m_new, l * corr + p.sum(-1), acc


def decode_attention(q, k_new, v_new, lf_new, load_past, n_blocks):
    bd, t, n_kv, n_grp, dk = q.shape
    dv = v_new.shape[-1]
    scale = dk ** -0.5
    use_forget = lf_new is not None
    if use_forget:
        g_new = jnp.transpose(jnp.cumsum(lf_new, axis=1), (0, 2, 3, 1))
        g_q = g_new[..., :, None]

    def past_block(carry, j):
        m, l, acc = carry[:3]
        k, v, lf = load_past(j)
        s = jnp.einsum('bthgd,bshd->bhgts', q, k).astype(F32) * scale
        extra = ()
        if use_forget:
            suf = carry[3]
            tot = lf.sum(axis=1)
            s_past = suf[:, None] + tot[:, None] - jnp.cumsum(lf, axis=1)
            s = s + jnp.transpose(s_past, (0, 2, 3, 1))[:, :, :, None, :] + g_q
            extra = (suf + tot,)
        m, l, acc = _online_update(m, l, acc, s, v)
        return (m, l, acc) + extra, None

    init = (jnp.full((bd, n_kv, n_grp, t), -jnp.inf, F32), jnp.zeros((bd, n_kv, n_grp, t), F32),
            jnp.zeros((bd, n_kv, n_grp, t, dv), F32))
    if use_forget:
        init = init + (jnp.zeros((bd, n_kv, n_grp), F32),)
    carry, _ = lax.scan(past_block, init, jnp.arange(n_blocks), reverse=True)
    m, l, acc = carry[:3]
    s = jnp.einsum('bthgd,bshd->bhgts', q, k_new).astype(F32) * scale
    if use_forget:
        s = s + g_q - g_new[..., None, :]
    causal = jnp.tril(jnp.ones((t, t), bool))
    s = jnp.where(causal, s, -jnp.inf)
    m, l, acc = _online_update(m, l, acc, s, v_new)
    out = acc / l[..., None]
    return jnp.transpose(out, (0, 3, 1, 2, 4)).astype(v_new.dtype)


def mlstm_chunkwise(q, k, v, ig, lf, c0, n0, m0):
    bsz, seq, nh, dk = q.shape
    dv = v.shape[-1]
    ln = math.gcd(seq, MLSTM_CHUNK)
    nc = seq // ln
    q = q.astype(F32)
    k = k.astype(F32) * dk ** -0.5
    v = v.astype(F32)

    def chunks(a):
        return jnp.moveaxis(a.reshape((bsz, nc, ln) + a.shape[2:]), 1, 0)

    causal = jnp.tril(jnp.ones((ln, ln), bool))[None, :, :, None]

    def step(carry, inp):
        c, n, m = carry
        qc, kc, vc, ic, fc = inp
        b = jnp.cumsum(fc, axis=1)
        dmat = b[:, :, None, :] - b[:, None, :, :] + ic[:, None, :, :]
        dmat = jnp.where(causal, dmat, -jnp.inf)
        inter = b + m[:, None, :]
        mt = jnp.maximum(inter, dmat.max(axis=2))
        w = jnp.exp(dmat - mt[:, :, None, :])
        ei = jnp.exp(inter - mt)
        a = w * jnp.einsum('bthd,bshd->btsh', qc, kc)
        num = ei[..., None] * jnp.einsum('bhed,bthd->bthe', c, qc) + jnp.einsum('btsh,bshe->bthe', a, vc)
        den = ei * jnp.einsum('bhd,bthd->bth', n, qc) + a.sum(axis=2)
        h = num / jnp.maximum(jnp.abs(den), jnp.exp(-mt))[..., None]
        b_last = b[:, -1]
        wlog = b_last[:, None] - b + ic
        m_new = jnp.maximum(b_last + m, wlog.max(axis=1))
        ws = jnp.exp(wlog - m_new[:, None])
        decay = jnp.exp(b_last + m - m_new)
        c_new = decay[..., None, None] * c + jnp.einsum('bsh,bshe,bshd->bhed', ws, vc, kc)
        n_new = decay[..., None] * n + jnp.einsum('bsh,bshd->bhd', ws, kc)
        return (c_new, n_new, m_new), h

    init = (c0.astype(F32), n0.astype(F32), m0.astype(F32))
    (c1, n1, m1), hs = lax.scan(step, init, (chunks(q), chunks(k), chunks(v), chunks(ig), chunks(lf)))
    h = jnp.moveaxis(hs, 0, 1).reshape(bsz, seq, nh, dv)
    return h, c1, n1, m1


def s5_layer(u, a_re, a_im, log_step, b_re, b_im, c_re, c_im, d, w_glu, b_glu, x0_re, x0_im):
    bsz, seq, _ = u.shape
    a = lax.complex(a_re.astype(F32), a_im.astype(F32))
    step = jnp.exp(log_step.astype(F32))[:, None]
    a_bar = jnp.exp(a * step)
    b_bar = ((a_bar - 1.0) / a)[..., None] * lax.complex(b_re.astype(F32), b_im.astype(F32))
    ug = u.astype(F32).reshape(bsz, seq, S5_GROUPS, S5_GROUP_SIZE)
    bu = jnp.einsum('bsgi,gpi->bsgp', ug.astype(jnp.complex64), b_bar)
    x0 = lax.complex(x0_re.astype(F32), x0_im.astype(F32))
    bu = bu.at[:, 0].add(a_bar[None] * x0)
    a_all = jnp.broadcast_to(a_bar, bu.shape)

    def combine(lhs, rhs):
        a1, b1 = lhs
        a2, b2 = rhs
        return a1 * a2, a2 * b1 + b2

    _, xs = lax.associative_scan(combine, (a_all, bu), axis=1)
    cc = lax.complex(c_re.astype(F32), c_im.astype(F32))
    y = jnp.real(jnp.einsum('bsgp,gop->bsgo', xs, cc)) + ug * d.astype(F32).reshape(S5_GROUPS, S5_GROUP_SIZE)
    y = jax.nn.gelu(y.reshape(bsz, seq, S5_CH), approximate=False)
    out = y * jax.nn.sigmoid(y @ w_glu.astype(F32) + b_glu.astype(F32))
    x_last = xs[:, -1]
    return out.astype(u.dtype), jnp.real(x_last), jnp.imag(x_last)


def mla_keys(lat, krope, w_uk, w_uv, g_kn):
    k_nope = rmsnorm(jnp.einsum('bsr,rhd->bshd', lat, w_uk), g_kn)
    v = jnp.einsum('bsr,rhd->bshd', lat, w_uv)
    kr = jnp.broadcast_to(krope[:, :, None, :], k_nope.shape[:3] + (MLA_ROPE,)).astype(k_nope.dtype)
    return jnp.concatenate([k_nope, kr], axis=-1), v


def peer_ffn(c, w_q, subkeys, u_tab, v_tab):
    bsz, seq, dm = c.shape
    ntok = bsz * seq
    blk = math.gcd(ntok, PEER_TOKEN_BLOCK)
    half = PEER_KEY_DIM // 2

    def route_and_mix(xb):
        q = (xb @ w_q).reshape(blk, PEER_HEADS, 2, half)
        s = jnp.einsum('thcd,hcnd->thcn', q, subkeys).astype(F32)
        sv, si = lax.top_k(s, PEER_TOPK)
        cand = sv[:, :, 0, :, None] + sv[:, :, 1, None, :]
        cand_idx = si[:, :, 0, :, None] * PEER_NKEYS + si[:, :, 1, None, :]
        best, pos = lax.top_k(cand.reshape(blk, PEER_HEADS, PEER_TOPK * PEER_TOPK), PEER_TOPK)
        eidx = jnp.take_along_axis(cand_idx.reshape(blk, PEER_HEADS, PEER_TOPK * PEER_TOPK), pos, axis=-1)
        g = jax.nn.softmax(best, axis=-1)
        ue = u_tab[eidx]
        ve = v_tab[eidx]
        act = jax.nn.gelu(jnp.einsum('thkd,td->thk', ue, xb).astype(F32), approximate=False)
        return jnp.einsum('thk,thkd->td', (g * act).astype(xb.dtype), ve)

    out = lax.map(route_and_mix, c.reshape(ntok // blk, blk, dm))
    return out.reshape(bsz, seq, dm)


def mixing_sublayer(a, pos, lp, past):
    bsz, seq, _ = a.shape
    z = a @ lp['w_in']
    points = [int(t) for t in np.cumsum(SPLIT_SIZES)[:-1]]
    (fq, fk, fv, ff, mq, mk, mv, mi, mf, mo, su, cq, ckv, kr) = jnp.split(z, points, axis=-1)

    q_f = rmsnorm(fq.reshape(bsz, seq, FOX_KV_HEADS, FOX_GROUP, HEAD_DIM), lp['g_fox_q'])
    k_f = rmsnorm(fk.reshape(bsz, seq, FOX_KV_HEADS, HEAD_DIM), lp['g_fox_k'])
    v_f = fv.reshape(bsz, seq, FOX_KV_HEADS, HEAD_DIM)
    lf_f = jax.nn.log_sigmoid(ff.astype(F32) + lp['b_fox_f'].astype(F32))
    lf_g = lf_f.reshape(bsz, seq, FOX_KV_HEADS, FOX_GROUP)

    c_q = rmsnorm(cq, lp['g_mla_cq'])
    q_m = jnp.einsum('bsr,rhd->bshd', c_q, lp['w_mla_uq'])
    q_m = jnp.concatenate([rmsnorm(q_m[..., :MLA_NOPE], lp['g_mla_qn']),
                           rope(rmsnorm(q_m[..., MLA_NOPE:], lp['g_mla_qr']), pos)], axis=-1)[:, :, :, None, :]
    lat = rmsnorm(ckv, lp['g_mla_ckv'])
    krope = rope(rmsnorm(kr, lp['g_mla_kr'])[:, :, None, :], pos)[:, :, 0, :]
    k_m, v_m = mla_keys(lat, krope, lp['w_mla_uk'], lp['w_mla_uv'], lp['g_mla_kn'])

    if past is None:
        o_f = prompt_attention(q_f, k_f, v_f, jnp.cumsum(lf_g, axis=1))
        o_m = prompt_attention(q_m, k_m, v_m, None)
        c0 = jnp.zeros((bsz, MLSTM_HEADS, HEAD_DIM, HEAD_DIM), F32)
        n0 = jnp.zeros((bsz, MLSTM_HEADS, HEAD_DIM), F32)
        m0 = jnp.zeros((bsz, MLSTM_HEADS), F32)
        x0_re = jnp.zeros((bsz, S5_GROUPS, S5_STATE), F32)
        x0_im = jnp.zeros((bsz, S5_GROUPS, S5_STATE), F32)
    else:
        o_f = decode_attention(q_f, k_f, v_f, lf_g, past['load_fox'], past['n_blocks'])
        o_m = decode_attention(q_m, k_m, v_m, None, past['load_mla'], past['n_blocks'])
        c0, n0, m0, x0_re, x0_im = past['rec']

    ig = mi.astype(F32) + lp['b_mlstm_i'].astype(F32)
    lf_l = jax.nn.log_sigmoid(mf.astype(F32) + lp['b_mlstm_f'].astype(F32))
    h_l, c1, n1, m1 = mlstm_chunkwise(mq.reshape(bsz, seq, MLSTM_HEADS, HEAD_DIM),
                                      mk.reshape(bsz, seq, MLSTM_HEADS, HEAD_DIM),
                                      mv.reshape(bsz, seq, MLSTM_HEADS, HEAD_DIM), ig, lf_l, c0, n0, m0)
    o_l = rmsnorm(h_l, lp['g_mlstm_h']).astype(a.dtype).reshape(bsz, seq, MLSTM_W) * jax.nn.sigmoid(mo)

    o_s, s_re, s_im = s5_layer(su, lp['s5_a_re'], lp['s5_a_im'], lp['s5_log_step'], lp['s5_b_re'], lp['s5_b_im'],
                               lp['s5_c_re'], lp['s5_c_im'], lp['s5_d'], lp['w_glu'], lp['b_glu'], x0_re, x0_im)

    mix = jnp.concatenate([o_f.reshape(bsz, seq, -1), o_l, o_s, o_m.reshape(bsz, seq, -1)], axis=-1)
    out = mix @ lp['w_out']
    return out, (k_f, v_f, lf_f, lat, krope, c1, n1, m1, s_re, s_im)


def run_trunk(x, p, pos, layer_params, pasts):
    h = x
    states = []
    for i in range(DEPTH):
        lp = layer_params[i]
        mix, st = mixing_sublayer(rmsnorm(h, lp['g_mix']), pos, lp, pasts[i])
        h = h + mix
        h = h + peer_ffn(rmsnorm(h, lp['g_ffn']), lp['w_peer_q'], lp['peer_subkeys'], lp['peer_u'], lp['peer_v'])
        gate = jax.nn.sigmoid(rmsnorm(h, lp['g_ple']) @ lp['w_ple_gate'])
        h = h + (p[i] @ lp['w_ple']) * gate
        states.append(st)
    stacked = [jnp.stack([st[j] for st in states], axis=0) for j in range(len(states[0]))]
    return h, stacked


def setup_inputs(seed: int = 0) -> dict:
    key = jax.random.key(seed)
    ks = iter(jax.random.split(key, 64))

    def nrm(shape, scale=1.0):
        return jax.random.normal(next(ks), shape, F32) * scale

    def gain(shape):
        return 1.0 + 0.02 * jax.random.normal(next(ks), shape, F32)

    L, DB = DEPTH, DEC_BATCH
    n_pages = PAST_LEN // PAGE_SIZE
    n_used = DEC_BATCH * n_pages
    n_pool = n_used + max(1, n_used // 4)
    inp = {}
    inp['x_prompt'] = nrm((BATCH, SEQ, D_MODEL))
    inp['x_sample'] = nrm((DEC_BATCH, DEC_SEQ, D_MODEL))
    inp['cache_fox_k'] = nrm((L, n_pool, PAGE_SIZE, FOX_KV_HEADS, HEAD_DIM))
    inp['cache_fox_v'] = nrm((L, n_pool, PAGE_SIZE, FOX_KV_HEADS, HEAD_DIM))
    inp['cache_fox_logf'] = jax.nn.log_sigmoid(FORGET_BIAS + nrm((L, n_pool, PAGE_SIZE, FOX_HEADS)))
    inp['cache_mla_latent'] = nrm((L, n_pool, PAGE_SIZE, MLA_KV_RANK))
    inp['cache_mla_krope'] = nrm((L, n_pool, PAGE_SIZE, MLA_ROPE))
    inp['state_mlstm_C'] = nrm((L, DB, MLSTM_HEADS, HEAD_DIM, HEAD_DIM), 0.1)
    inp['state_mlstm_n'] = nrm((L, DB, MLSTM_HEADS, HEAD_DIM), 0.1)
    inp['state_mlstm_m'] = nrm((L, DB, MLSTM_HEADS))
    inp['state_s5_re'] = nrm((L, DB, S5_GROUPS, S5_STATE), 0.1)
    inp['state_s5_im'] = nrm((L, DB, S5_GROUPS, S5_STATE), 0.1)
    inp['page_table'] = jax.random.permutation(next(ks), n_pool)[:n_used].reshape(DEC_BATCH, n_pages).astype(jnp.int32)
    inp['p_prompt'] = nrm((L, BATCH, SEQ, PLE_DIM))
    inp['p_sample'] = nrm((L, DEC_BATCH, DEC_SEQ, PLE_DIM))
    inp['g_mix'] = gain((L, D_MODEL))
    inp['w_in'] = nrm((L, D_MODEL, D_IN), D_MODEL ** -0.5)
    inp['g_fox_q'] = gain((L, HEAD_DIM))
    inp['g_fox_k'] = gain((L, HEAD_DIM))
    inp['b_fox_f'] = FORGET_BIAS + nrm((L, FOX_HEADS), 0.1)
    inp['b_mlstm_i'] = nrm((L, MLSTM_HEADS), 0.1)
    inp['b_mlstm_f'] = FORGET_BIAS + nrm((L, MLSTM_HEADS), 0.1)
    inp['g_mlstm_h'] = gain((L, HEAD_DIM))
    inp['s5_a_re'] = -0.5 + nrm((L, S5_GROUPS, S5_STATE), 0.01)
    inp['s5_a_im'] = math.pi * jnp.broadcast_to(jnp.arange(S5_STATE, dtype=F32), (L, S5_GROUPS, S5_STATE)) + nrm((L, S5_GROUPS, S5_STATE), 0.01)
    inp['s5_log_step'] = jax.random.uniform(next(ks), (L, S5_GROUPS), F32, minval=math.log(1e-3), maxval=math.log(1e-1))
    inp['s5_b_re'] = nrm((L, S5_GROUPS, S5_STATE, S5_GROUP_SIZE), (2 * S5_GROUP_SIZE) ** -0.5)
    inp['s5_b_im'] = nrm((L, S5_GROUPS, S5_STATE, S5_GROUP_SIZE), (2 * S5_GROUP_SIZE) ** -0.5)
    inp['s5_c_re'] = nrm((L, S5_GROUPS, S5_GROUP_SIZE, S5_STATE), 0.5)
    inp['s5_c_im'] = nrm((L, S5_GROUPS, S5_GROUP_SIZE, S5_STATE), 0.5)
    inp['s5_d'] = nrm((L, S5_CH))
    inp['w_glu'] = nrm((L, S5_CH, S5_CH), S5_CH ** -0.5)
    inp['b_glu'] = nrm((L, S5_CH), 0.01)
    inp['g_mla_cq'] = gain((L, MLA_Q_RANK))
    inp['w_mla_uq'] = nrm((L, MLA_Q_RANK, MLA_HEADS, MLA_NOPE + MLA_ROPE), MLA_Q_RANK ** -0.5)
    inp['g_mla_qn'] = gain((L, MLA_NOPE))
    inp['g_mla_qr'] = gain((L, MLA_ROPE))
    inp['g_mla_ckv'] = gain((L, MLA_KV_RANK))
    inp['g_mla_kr'] = gain((L, MLA_ROPE))
    inp['w_mla_uk'] = nrm((L, MLA_KV_RANK, MLA_HEADS, MLA_NOPE), MLA_KV_RANK ** -0.5)
    inp['w_mla_uv'] = nrm((L, MLA_KV_RANK, MLA_HEADS, MLA_V), MLA_KV_RANK ** -0.5)
    inp['g_mla_kn'] = gain((L, MLA_NOPE))
    inp['w_out'] = nrm((L, MIX_WIDTH, D_MODEL), MIX_WIDTH ** -0.5)
    inp['g_ffn'] = gain((L, D_MODEL))
    inp['w_peer_q'] = nrm((L, D_MODEL, PEER_HEADS * PEER_KEY_DIM), D_MODEL ** -0.5)
    inp['peer_subkeys'] = nrm((L, PEER_HEADS, 2, PEER_NKEYS, PEER_KEY_DIM // 2), (PEER_KEY_DIM // 2) ** -0.5)
    inp['peer_u'] = nrm((L, PEER_EXPERTS, D_MODEL), D_MODEL ** -0.5)
    inp['peer_v'] = nrm((L, PEER_EXPERTS, D_MODEL), PEER_HEADS ** -0.5)
    inp['g_ple'] = gain((L, D_MODEL))
    inp['w_ple_gate'] = nrm((L, D_MODEL, D_MODEL), D_MODEL ** -0.5)
    inp['w_ple'] = nrm((L, PLE_DIM, D_MODEL), PLE_DIM ** -0.5)
    return inp


def reference(x_prompt, x_sample, cache_fox_k, cache_fox_v, cache_fox_logf, cache_mla_latent, cache_mla_krope,
              state_mlstm_C, state_mlstm_n, state_mlstm_m, state_s5_re, state_s5_im, page_table, p_prompt, p_sample,
              g_mix, w_in, g_fox_q, g_fox_k, b_fox_f, b_mlstm_i, b_mlstm_f, g_mlstm_h,
              s5_a_re, s5_a_im, s5_log_step, s5_b_re, s5_b_im, s5_c_re, s5_c_im, s5_d, w_glu, b_glu,
              g_mla_cq, w_mla_uq, g_mla_qn, g_mla_qr, g_mla_ckv, g_mla_kr, w_mla_uk, w_mla_uv, g_mla_kn,
              w_out, g_ffn, w_peer_q, peer_subkeys, peer_u, peer_v, g_ple, w_ple_gate, w_ple):
    layer_params = []
    for i in range(DEPTH):
        layer_params.append({
            'g_mix': g_mix[i], 'w_in': w_in[i], 'g_fox_q': g_fox_q[i], 'g_fox_k': g_fox_k[i], 'b_fox_f': b_fox_f[i],
            'b_mlstm_i': b_mlstm_i[i], 'b_mlstm_f': b_mlstm_f[i], 'g_mlstm_h': g_mlstm_h[i],
            's5_a_re': s5_a_re[i], 's5_a_im': s5_a_im[i], 's5_log_step': s5_log_step[i],
            's5_b_re': s5_b_re[i], 's5_b_im': s5_b_im[i], 's5_c_re': s5_c_re[i], 's5_c_im': s5_c_im[i],
            's5_d': s5_d[i], 'w_glu': w_glu[i], 'b_glu': b_glu[i],
            'g_mla_cq': g_mla_cq[i], 'w_mla_uq': w_mla_uq[i], 'g_mla_qn': g_mla_qn[i], 'g_mla_qr': g_mla_qr[i],
            'g_mla_ckv': g_mla_ckv[i], 'g_mla_kr': g_mla_kr[i], 'w_mla_uk': w_mla_uk[i], 'w_mla_uv': w_mla_uv[i],
            'g_mla_kn': g_mla_kn[i], 'w_out': w_out[i], 'g_ffn': g_ffn[i], 'w_peer_q': w_peer_q[i],
            'peer_subkeys': peer_subkeys[i], 'peer_u': peer_u[i], 'peer_v': peer_v[i],
            'g_ple': g_ple[i], 'w_ple_gate': w_ple_gate[i], 'w_ple': w_ple[i]})

    pos_p = jnp.arange(x_prompt.shape[1], dtype=jnp.int32)
    y_prompt, st_p = run_trunk(x_prompt, p_prompt, pos_p, layer_params, [None] * DEPTH)

    n_pages = page_table.shape[1]
    past_len = n_pages * PAGE_SIZE
    bp = math.gcd(n_pages, MAX_KEY_BLOCK_PAGES)
    n_blocks = n_pages // bp
    rows = bp * PAGE_SIZE
    db = x_sample.shape[0]

    def make_past(i):
        def pages_of(j):
            return lax.dynamic_slice_in_dim(page_table, j * bp, bp, axis=1)

        def load_fox(j):
            pg = pages_of(j)
            k = cache_fox_k[i, pg].reshape(db, rows, FOX_KV_HEADS, HEAD_DIM)
            v = cache_fox_v[i, pg].reshape(db, rows, FOX_KV_HEADS, HEAD_DIM)
            lf = cache_fox_logf[i, pg].reshape(db, rows, FOX_KV_HEADS, FOX_GROUP).astype(F32)
            return k, v, lf

        def load_mla(j):
            pg = pages_of(j)
            lat = cache_mla_latent[i, pg].reshape(db, rows, MLA_KV_RANK)
            kr = cache_mla_krope[i, pg].reshape(db, rows, MLA_ROPE)
            k, v = mla_keys(lat, kr, w_mla_uk[i], w_mla_uv[i], g_mla_kn[i])
            return k, v, None

        return {'load_fox': load_fox, 'load_mla': load_mla, 'n_blocks': n_blocks,
                'rec': (state_mlstm_C[i], state_mlstm_n[i], state_mlstm_m[i], state_s5_re[i], state_s5_im[i])}

    pasts = [make_past(i) for i in range(DEPTH)]
    pos_s = past_len + jnp.arange(x_sample.shape[1], dtype=jnp.int32)
    y_sample, st_s = run_trunk(x_sample, p_sample, pos_s, layer_params, pasts)

    (pk, pv, plf, plat, pkr, pc, pn, pm, pre, pim) = st_p
    (sk, sv, slf, slat, skr, sc, sn, sm, sre, sim) = st_s
    return (y_prompt, y_sample, pk, pv, plf, plat, pkr, pc, pn, pm, pre, pim,
            sk, sv, slf, slat, skr, sc, sn, sm, sre, sim)
```

```python
import functools
import math

import jax
import jax.numpy as jnp
import numpy as np
from jax import lax
from jax.experimental import pallas as pl
from jax.experimental.pallas import tpu as pltpu

D_MODEL = 1024
DEPTH = 2
PAGE_SIZE = 128
HEAD_DIM = 64
GROUP_WIDTH = D_MODEL // 4
FOX_HEADS = GROUP_WIDTH // HEAD_DIM
FOX_KV_HEADS = FOX_HEADS // 2
FOX_GROUP = FOX_HEADS // FOX_KV_HEADS
MLSTM_HEADS = GROUP_WIDTH // HEAD_DIM
MLSTM_W = MLSTM_HEADS * HEAD_DIM
MLSTM_CHUNK = 64
S5_CH = GROUP_WIDTH
S5_GROUP_SIZE = 16
S5_GROUPS = S5_CH // S5_GROUP_SIZE
S5_STATE = 64
MLA_HEADS = GROUP_WIDTH // HEAD_DIM
MLA_NOPE = HEAD_DIM
MLA_ROPE = HEAD_DIM // 2
MLA_V = HEAD_DIM
MLA_Q_RANK = D_MODEL // 8
MLA_KV_RANK = D_MODEL // 8
ROPE_THETA = 10000.0
PEER_HEADS = 8
PEER_NKEYS = 128
PEER_KEY_DIM = 256
PEER_TOPK = 16
PEER_TOKEN_BLOCK = 256
Q_BLOCK = 128
MAX_KEY_BLOCK_PAGES = 8
EPS = 1e-6
F32 = jnp.float32
BF16 = jnp.bfloat16

SPLIT_SIZES = (FOX_HEADS * HEAD_DIM, FOX_KV_HEADS * HEAD_DIM, FOX_KV_HEADS * HEAD_DIM, FOX_HEADS,
               MLSTM_W, MLSTM_W, MLSTM_W, MLSTM_HEADS, MLSTM_HEADS, MLSTM_W,
               S5_CH,
               MLA_Q_RANK, MLA_KV_RANK, MLA_ROPE)

V7X_LANES = 128
V7X_VMEM_LIMIT = 48 * 1024 * 1024


def _mm_kernel(x_ref, w_ref, o_ref):
    o_ref[...] = jnp.dot(x_ref[...].astype(BF16), w_ref[...], preferred_element_type=F32)


def _row_tile(n):
    for t in (512, 256, 128, 64, 32, 16, 8):
        if n % t == 0:
            return t
    raise ValueError(f"row count {n} is not a multiple of 8")


def matmul(x, w):
    n, k = x.shape
    m = w.shape[1]
    m_pad = -(-m // V7X_LANES) * V7X_LANES
    wb = w.astype(BF16)
    if m_pad != m:
        wb = jnp.pad(wb, ((0, 0), (0, m_pad - m)))
    tm = _row_tile(n)
    out = pl.pallas_call(
        _mm_kernel,
        grid=(n // tm,),
        in_specs=[pl.BlockSpec((tm, k), lambda i: (i, 0)),
                  pl.BlockSpec((k, m_pad), lambda i: (0, 0))],
        out_specs=pl.BlockSpec((tm, m_pad), lambda i: (i, 0)),
        out_shape=jax.ShapeDtypeStruct((n, m_pad), F32),
        compiler_params=pltpu.CompilerParams(dimension_semantics=("arbitrary",),
                                             vmem_limit_bytes=V7X_VMEM_LIMIT),
        name="matmul",
    )(x, wb)
    return out[:, :m] if m_pad != m else out


def matmul3(x, w):
    b, s, k = x.shape
    return matmul(x.reshape(b * s, k), w).reshape(b, s, w.shape[1])


def rmsnorm(x, g):
    xf = x.astype(F32)
    y = xf * lax.rsqrt(jnp.mean(xf * xf, axis=-1, keepdims=True) + EPS)
    return (y * g.astype(F32)).astype(x.dtype)


def rope(x, pos):
    half = x.shape[-1] // 2
    inv_freq = ROPE_THETA ** (-jnp.arange(half, dtype=F32) / half)
    ang = pos.astype(F32)[:, None] * inv_freq[None, :]
    cos = jnp.cos(ang)[None, :, None, :]
    sin = jnp.sin(ang)[None, :, None, :]
    xf = x.astype(F32)
    x1, x2 = xf[..., :half], xf[..., half:]
    return jnp.concatenate([x1 * cos - x2 * sin, x2 * cos + x1 * sin], axis=-1).astype(x.dtype)


def prompt_attention(q, k, v, fcum):
    bsz, seq, n_kv, n_grp, dk = q.shape
    dv = v.shape[-1]
    qb = math.gcd(seq, Q_BLOCK)
    scale = dk ** -0.5
    kpos = jnp.arange(seq)
    ft = None if fcum is None else jnp.transpose(fcum, (0, 2, 3, 1))

    def one_block(blk):
        start = blk * qb
        qblk = lax.dynamic_slice_in_dim(q, start, qb, axis=1)
        s = jnp.einsum('bqhgd,bkhd->bhgqk', qblk, k).astype(F32) * scale
        if ft is not None:
            fq = lax.dynamic_slice_in_dim(ft, start, qb, axis=3)
            s = s + fq[..., :, None] - ft[..., None, :]
        qpos = start + jnp.arange(qb)
        s = jnp.where(qpos[:, None] >= kpos[None, :], s, -jnp.inf)
        p = jax.nn.softmax(s, axis=-1)
        return jnp.einsum('bhgqk,bkhe->bqhge', p.astype(v.dtype), v)

    out = lax.map(one_block, jnp.arange(seq // qb))
    return jnp.moveaxis(out, 0, 1).reshape(bsz, seq, n_kv, n_grp, dv)


def _online_update(m, l, acc, s, v):
    m_new = jnp.maximum(m, s.max(-1))
    p = jnp.exp(s - m_new[..., None])
    corr = jnp.exp(m - m_new)
    acc = acc * corr[..., None] + jnp.einsum('bhgts,bshe->bhgte', p, v.astype(F32))
    return m_new, l * corr + p.sum(-1), acc


def decode_attention(q, k_new, v_new, lf_new, load_past, n_blocks):
    bd, t, n_kv, n_grp, dk = q.shape
    dv = v_new.shape[-1]
    scale = dk ** -0.5
    use_forget = lf_new is not None
    if use_forget:
        g_new = jnp.transpose(jnp.cumsum(lf_new, axis=1), (0, 2, 3, 1))
        g_q = g_new[..., :, None]

    def past_block(carry, j):
        m, l, acc = carry[:3]
        k, v, lf = load_past(j)
        s = jnp.einsum('bthgd,bshd->bhgts', q, k).astype(F32) * scale
        extra = ()
        if use_forget:
            suf = carry[3]
            tot = lf.sum(axis=1)
            s_past = suf[:, None] + tot[:, None] - jnp.cumsum(lf, axis=1)
            s = s + jnp.transpose(s_past, (0, 2, 3, 1))[:, :, :, None, :] + g_q
            extra = (suf + tot,)
        m, l, acc = _online_update(m, l, acc, s, v)
        return (m, l, acc) + extra, None

    init = (jnp.full((bd, n_kv, n_grp, t), -jnp.inf, F32), jnp.zeros((bd, n_kv, n_grp, t), F32),
            jnp.zeros((bd, n_kv, n_grp, t, dv), F32))
    if use_forget:
        init = init + (jnp.zeros((bd, n_kv, n_grp), F32),)
    carry, _ = lax.scan(past_block, init, jnp.arange(n_blocks), reverse=True)
    m, l, acc = carry[:3]
    s = jnp.einsum('bthgd,bshd->bhgts', q, k_new).astype(F32) * scale
    if use_forget:
        s = s + g_q - g_new[..., None, :]
    causal = jnp.tril(jnp.ones((t, t), bool))
    s = jnp.where(causal, s, -jnp.inf)
    m, l, acc = _online_update(m, l, acc, s, v_new)
    out = acc / l[..., None]
    return jnp.transpose(out, (0, 3, 1, 2, 4)).astype(v_new.dtype)


def mlstm_chunkwise(q, k, v, ig, lf, c0, n0, m0):
    bsz, seq, nh, dk = q.shape
    dv = v.shape[-1]
    ln = math.gcd(seq, MLSTM_CHUNK)
    nc = seq // ln
    q = q.astype(F32)
    k = k.astype(F32) * dk ** -0.5
    v = v.astype(F32)

    def chunks(a):
        return jnp.moveaxis(a.reshape((bsz, nc, ln) + a.shape[2:]), 1, 0)

    causal = jnp.tril(jnp.ones((ln, ln), bool))[None, :, :, None]

    def step(carry, inp):
        c, n, m = carry
        qc, kc, vc, ic, fc = inp
        b = jnp.cumsum(fc, axis=1)
        dmat = b[:, :, None, :] - b[:, None, :, :] + ic[:, None, :, :]
        dmat = jnp.where(causal, dmat, -jnp.inf)
        inter = b + m[:, None, :]
        mt = jnp.maximum(inter, dmat.max(axis=2))
        w = jnp.exp(dmat - mt[:, :, None, :])
        ei = jnp.exp(inter - mt)
        a = w * jnp.einsum('bthd,bshd->btsh', qc, kc)
        num = ei[..., None] * jnp.einsum('bhed,bthd->bthe', c, qc) + jnp.einsum('btsh,bshe->bthe', a, vc)
        den = ei * jnp.einsum('bhd,bthd->bth', n, qc) + a.sum(axis=2)
        h = num / jnp.maximum(jnp.abs(den), jnp.exp(-mt))[..., None]
        b_last = b[:, -1]
        wlog = b_last[:, None] - b + ic
        m_new = jnp.maximum(b_last + m, wlog.max(axis=1))
        ws = jnp.exp(wlog - m_new[:, None])
        decay = jnp.exp(b_last + m - m_new)
        c_new = decay[..., None, None] * c + jnp.einsum('bsh,bshe,bshd->bhed', ws, vc, kc)
        n_new = decay[..., None] * n + jnp.einsum('bsh,bshd->bhd', ws, kc)
        return (c_new, n_new, m_new), h

    init = (c0.astype(F32), n0.astype(F32), m0.astype(F32))
    (c1, n1, m1), hs = lax.scan(step, init, (chunks(q), chunks(k), chunks(v), chunks(ig), chunks(lf)))
    h = jnp.moveaxis(hs, 0, 1).reshape(bsz, seq, nh, dv)
    return h, c1, n1, m1


def s5_layer(u, a_re, a_im, log_step, b_re, b_im, c_re, c_im, d, w_glu, b_glu, x0_re, x0_im):
    bsz, seq, _ = u.shape
    a = lax.complex(a_re.astype(F32), a_im.astype(F32))
    step = jnp.exp(log_step.astype(F32))[:, None]
    a_bar = jnp.exp(a * step)
    b_bar = ((a_bar - 1.0) / a)[..., None] * lax.complex(b_re.astype(F32), b_im.astype(F32))
    ug = u.astype(F32).reshape(bsz, seq, S5_GROUPS, S5_GROUP_SIZE)
    bu = jnp.einsum('bsgi,gpi->bsgp', ug.astype(jnp.complex64), b_bar)
    x0 = lax.complex(x0_re.astype(F32), x0_im.astype(F32))
    bu = bu.at[:, 0].add(a_bar[None] * x0)
    a_all = jnp.broadcast_to(a_bar, bu.shape)

    def combine(lhs, rhs):
        a1, b1 = lhs
        a2, b2 = rhs
        return a1 * a2, a2 * b1 + b2

    _, xs = lax.associative_scan(combine, (a_all, bu), axis=1)
    cc = lax.complex(c_re.astype(F32), c_im.astype(F32))
    y = jnp.real(jnp.einsum('bsgp,gop->bsgo', xs, cc)) + ug * d.astype(F32).reshape(S5_GROUPS, S5_GROUP_SIZE)
    y = jax.nn.gelu(y.reshape(bsz, seq, S5_CH), approximate=False)
    out = y * jax.nn.sigmoid(y @ w_glu.astype(F32) + b_glu.astype(F32))
    x_last = xs[:, -1]
    return out.astype(u.dtype), jnp.real(x_last), jnp.imag(x_last)


def mla_keys(lat, krope, w_uk, w_uv, g_kn):
    k_nope = rmsnorm(jnp.einsum('bsr,rhd->bshd', lat, w_uk), g_kn)
    v = jnp.einsum('bsr,rhd->bshd', lat, w_uv)
    kr = jnp.broadcast_to(krope[:, :, None, :], k_nope.shape[:3] + (MLA_ROPE,)).astype(k_nope.dtype)
    return jnp.concatenate([k_nope, kr], axis=-1), v


def peer_ffn(c, w_q, subkeys, u_tab, v_tab):
    bsz, seq, dm = c.shape
    ntok = bsz * seq
    blk = math.gcd(ntok, PEER_TOKEN_BLOCK)
    half = PEER_KEY_DIM // 2
    qall = matmul(c.reshape(ntok, dm), w_q).reshape(ntok // blk, blk, PEER_HEADS * PEER_KEY_DIM)

    def route_and_mix(args):
        xb, qb = args
        q = qb.reshape(blk, PEER_HEADS, 2, half)
        s = jnp.einsum('thcd,hcnd->thcn', q, subkeys).astype(F32)
        sv, si = lax.top_k(s, PEER_TOPK)
        cand = sv[:, :, 0, :, None] + sv[:, :, 1, None, :]
        cand_idx = si[:, :, 0, :, None] * PEER_NKEYS + si[:, :, 1, None, :]
        best, pos = lax.top_k(cand.reshape(blk, PEER_HEADS, PEER_TOPK * PEER_TOPK), PEER_TOPK)
        eidx = jnp.take_along_axis(cand_idx.reshape(blk, PEER_HEADS, PEER_TOPK * PEER_TOPK), pos, axis=-1)
        g = jax.nn.softmax(best, axis=-1)
        ue = u_tab[eidx]
        ve = v_tab[eidx]
        act = jax.nn.gelu(jnp.einsum('thkd,td->thk', ue, xb).astype(F32), approximate=False)
        return jnp.einsum('thk,thkd->td', (g * act).astype(xb.dtype), ve)

    out = lax.map(route_and_mix, (c.reshape(ntok // blk, blk, dm), qall))
    return out.reshape(bsz, seq, dm)


def mixing_sublayer(a, pos, lp, past):
    bsz, seq, _ = a.shape
    z = matmul3(a, lp['w_in'])
    points = [int(t) for t in np.cumsum(SPLIT_SIZES)[:-1]]
    (fq, fk, fv, ff, mq, mk, mv, mi, mf, mo, su, cq, ckv, kr) = jnp.split(z, points, axis=-1)

    q_f = rmsnorm(fq.reshape(bsz, seq, FOX_KV_HEADS, FOX_GROUP, HEAD_DIM), lp['g_fox_q'])
    k_f = rmsnorm(fk.reshape(bsz, seq, FOX_KV_HEADS, HEAD_DIM), lp['g_fox_k'])
    v_f = fv.reshape(bsz, seq, FOX_KV_HEADS, HEAD_DIM)
    lf_f = jax.nn.log_sigmoid(ff.astype(F32) + lp['b_fox_f'].astype(F32))
    lf_g = lf_f.reshape(bsz, seq, FOX_KV_HEADS, FOX_GROUP)

    c_q = rmsnorm(cq, lp['g_mla_cq'])
    q_m = jnp.einsum('bsr,rhd->bshd', c_q, lp['w_mla_uq'])
    q_m = jnp.concatenate([rmsnorm(q_m[..., :MLA_NOPE], lp['g_mla_qn']),
                           rope(rmsnorm(q_m[..., MLA_NOPE:], lp['g_mla_qr']), pos)], axis=-1)[:, :, :, None, :]
    lat = rmsnorm(ckv, lp['g_mla_ckv'])
    krope = rope(rmsnorm(kr, lp['g_mla_kr'])[:, :, None, :], pos)[:, :, 0, :]
    k_m, v_m = mla_keys(lat, krope, lp['w_mla_uk'], lp['w_mla_uv'], lp['g_mla_kn'])

    if past is None:
        o_f = prompt_attention(q_f, k_f, v_f, jnp.cumsum(lf_g, axis=1))
        o_m = prompt_attention(q_m, k_m, v_m, None)
        c0 = jnp.zeros((bsz, MLSTM_HEADS, HEAD_DIM, HEAD_DIM), F32)
        n0 = jnp.zeros((bsz, MLSTM_HEADS, HEAD_DIM), F32)
        m0 = jnp.zeros((bsz, MLSTM_HEADS), F32)
        x0_re = jnp.zeros((bsz, S5_GROUPS, S5_STATE), F32)
        x0_im = jnp.zeros((bsz, S5_GROUPS, S5_STATE), F32)
    else:
        o_f = decode_attention(q_f, k_f, v_f, lf_g, past['load_fox'], past['n_blocks'])
        o_m = decode_attention(q_m, k_m, v_m, None, past['load_mla'], past['n_blocks'])
        c0, n0, m0, x0_re, x0_im = past['rec']

    ig = mi.astype(F32) + lp['b_mlstm_i'].astype(F32)
    lf_l = jax.nn.log_sigmoid(mf.astype(F32) + lp['b_mlstm_f'].astype(F32))
    h_l, c1, n1, m1 = mlstm_chunkwise(mq.reshape(bsz, seq, MLSTM_HEADS, HEAD_DIM),
                                      mk.reshape(bsz, seq, MLSTM_HEADS, HEAD_DIM),
                                      mv.reshape(bsz, seq, MLSTM_HEADS, HEAD_DIM), ig, lf_l, c0, n0, m0)
    o_l = rmsnorm(h_l, lp['g_mlstm_h']).astype(a.dtype).reshape(bsz, seq, MLSTM_W) * jax.nn.sigmoid(mo)

    o_s, s_re, s_im = s5_layer(su, lp['s5_a_re'], lp['s5_a_im'], lp['s5_log_step'], lp['s5_b_re'], lp['s5_b_im'],
                               lp['s5_c_re'], lp['s5_c_im'], lp['s5_d'], lp['w_glu'], lp['b_glu'], x0_re, x0_im)

    mix = jnp.concatenate([o_f.reshape(bsz, seq, -1), o_l, o_s, o_m.reshape(bsz, seq, -1)], axis=-1)
    out = matmul3(mix, lp['w_out'])
    return out, (k_f, v_f, lf_f, lat, krope, c1, n1, m1, s_re, s_im)


def run_trunk(x, p, pos, layer_params, pasts):
    h = x
    states = []
    for i in range(DEPTH):
        lp = layer_params[i]
        mix, st = mixing_sublayer(rmsnorm(h, lp['g_mix']), pos, lp, pasts[i])
        h = h + mix
        h = h + peer_ffn(rmsnorm(h, lp['g_ffn']), lp['w_peer_q'], lp['peer_subkeys'], lp['peer_u'], lp['peer_v'])
        gate = jax.nn.sigmoid(matmul3(rmsnorm(h, lp['g_ple']), lp['w_ple_gate']))
        h = h + matmul3(p[i], lp['w_ple']) * gate
        states.append(st)
    stacked = [jnp.stack([st[j] for st in states], axis=0) for j in range(len(states[0]))]
    return h, stacked


def kernel(x_prompt, x_sample, cache_fox_k, cache_fox_v, cache_fox_logf, cache_mla_latent, cache_mla_krope,
           state_mlstm_C, state_mlstm_n, state_mlstm_m, state_s5_re, state_s5_im, page_table, p_prompt, p_sample,
           g_mix, w_in, g_fox_q, g_fox_k, b_fox_f, b_mlstm_i, b_mlstm_f, g_mlstm_h,
           s5_a_re, s5_a_im, s5_log_step, s5_b_re, s5_b_im, s5_c_re, s5_c_im, s5_d, w_glu, b_glu,
           g_mla_cq, w_mla_uq, g_mla_qn, g_mla_qr, g_mla_ckv, g_mla_kr, w_mla_uk, w_mla_uv, g_mla_kn,
           w_out, g_ffn, w_peer_q, peer_subkeys, peer_u, peer_v, g_ple, w_ple_gate, w_ple):
    names = ('g_mix', 'w_in', 'g_fox_q', 'g_fox_k', 'b_fox_f', 'b_mlstm_i', 'b_mlstm_f', 'g_mlstm_h',
             's5_a_re', 's5_a_im', 's5_log_step', 's5_b_re', 's5_b_im', 's5_c_re', 's5_c_im', 's5_d', 'w_glu',
             'b_glu', 'g_mla_cq', 'w_mla_uq', 'g_mla_qn', 'g_mla_qr', 'g_mla_ckv', 'g_mla_kr', 'w_mla_uk',
             'w_mla_uv', 'g_mla_kn', 'w_out', 'g_ffn', 'w_peer_q', 'peer_subkeys', 'peer_u', 'peer_v', 'g_ple',
             'w_ple_gate', 'w_ple')
    vals = (g_mix, w_in, g_fox_q, g_fox_k, b_fox_f, b_mlstm_i, b_mlstm_f, g_mlstm_h,
            s5_a_re, s5_a_im, s5_log_step, s5_b_re, s5_b_im, s5_c_re, s5_c_im, s5_d, w_glu,
            b_glu, g_mla_cq, w_mla_uq, g_mla_qn, g_mla_qr, g_mla_ckv, g_mla_kr, w_mla_uk,
            w_mla_uv, g_mla_kn, w_out, g_ffn, w_peer_q, peer_subkeys, peer_u, peer_v, g_ple,
            w_ple_gate, w_ple)
    layer_params = [{n: v[i] for n, v in zip(names, vals)} for i in range(DEPTH)]

    pos_p = jnp.arange(x_prompt.shape[1], dtype=jnp.int32)
    y_prompt, st_p = run_trunk(x_prompt, p_prompt, pos_p, layer_params, [None] * DEPTH)

    n_pages = page_table.shape[1]
    past_len = n_pages * PAGE_SIZE
    bp = math.gcd(n_pages, MAX_KEY_BLOCK_PAGES)
    n_blocks = n_pages // bp
    rows = bp * PAGE_SIZE
    db = x_sample.shape[0]

    def make_past(i):
        def pages_of(j):
            return lax.dynamic_slice_in_dim(page_table, j * bp, bp, axis=1)

        def load_fox(j):
            pg = pages_of(j)
            k = cache_fox_k[i, pg].reshape(db, rows, FOX_KV_HEADS, HEAD_DIM)
            v = cache_fox_v[i, pg].reshape(db, rows, FOX_KV_HEADS, HEAD_DIM)
            lf = cache_fox_logf[i, pg].reshape(db, rows, FOX_KV_HEADS, FOX_GROUP).astype(F32)
            return k, v, lf

        def load_mla(j):
            pg = pages_of(j)
            lat = cache_mla_latent[i, pg].reshape(db, rows, MLA_KV_RANK)
            kr = cache_mla_krope[i, pg].reshape(db, rows, MLA_ROPE)
            k, v = mla_keys(lat, kr, w_mla_uk[i], w_mla_uv[i], g_mla_kn[i])
            return k, v, None

        return {'load_fox': load_fox, 'load_mla': load_mla, 'n_blocks': n_blocks,
                'rec': (state_mlstm_C[i], state_mlstm_n[i], state_mlstm_m[i], state_s5_re[i], state_s5_im[i])}

    pasts = [make_past(i) for i in range(DEPTH)]
    pos_s = past_len + jnp.arange(x_sample.shape[1], dtype=jnp.int32)
    y_sample, st_s = run_trunk(x_sample, p_sample, pos_s, layer_params, pasts)

    (pk, pv, plf, plat, pkr, pc, pn, pm, pre, pim) = st_p
    (sk, sv, slf, slat, skr, sc, sn, sm, sre, sim) = st_s
    return (y_prompt, y_sample, pk, pv, plf, plat, pkr, pc, pn, pm, pre, pim,
            sk, sv, slf, slat, skr, sc, sn, sm, sre, sim)
```

```python
import functools
import math

import jax
import jax.numpy as jnp
import numpy as np
from jax import lax
from jax.experimental import pallas as pl
from jax.experimental.pallas import tpu as pltpu

D_MODEL = 1024
DEPTH = 2
PAGE_SIZE = 128
HEAD_DIM = 64
GROUP_WIDTH = D_MODEL // 4
FOX_HEADS = GROUP_WIDTH // HEAD_DIM
FOX_KV_HEADS = FOX_HEADS // 2
FOX_GROUP = FOX_HEADS // FOX_KV_HEADS
MLSTM_HEADS = GROUP_WIDTH // HEAD_DIM
MLSTM_W = MLSTM_HEADS * HEAD_DIM
MLSTM_CHUNK = 64
S5_CH = GROUP_WIDTH
S5_GROUP_SIZE = 16
S5_GROUPS = S5_CH // S5_GROUP_SIZE
S5_STATE = 64
MLA_HEADS = GROUP_WIDTH // HEAD_DIM
MLA_NOPE = HEAD_DIM
MLA_ROPE = HEAD_DIM // 2
MLA_V = HEAD_DIM
MLA_Q_RANK = D_MODEL // 8
MLA_KV_RANK = D_MODEL // 8
ROPE_THETA = 10000.0
PEER_HEADS = 8
PEER_NKEYS = 128
PEER_KEY_DIM = 256
PEER_TOPK = 16
PEER_TOKEN_BLOCK = 256
Q_BLOCK = 128
MAX_KEY_BLOCK_PAGES = 8
EPS = 1e-6
F32 = jnp.float32
BF16 = jnp.bfloat16

SPLIT_SIZES = (FOX_HEADS * HEAD_DIM, FOX_KV_HEADS * HEAD_DIM, FOX_KV_HEADS * HEAD_DIM, FOX_HEADS,
               MLSTM_W, MLSTM_W, MLSTM_W, MLSTM_HEADS, MLSTM_HEADS, MLSTM_W,
               S5_CH,
               MLA_Q_RANK, MLA_KV_RANK, MLA_ROPE)

V7X_LANES = 128
V7X_VMEM_LIMIT = 48 * 1024 * 1024


def _mm_kernel(x_ref, w_ref, o_ref):
    o_ref[...] = jnp.dot(x_ref[...].astype(BF16), w_ref[...], preferred_element_type=F32)


def _row_tile(n):
    for t in (512, 256, 128, 64, 32, 16, 8):
        if n % t == 0:
            return t
    raise ValueError(f"row count {n} is not a multiple of 8")


def matmul(x, w):
    n, k = x.shape
    m = w.shape[1]
    m_pad = -(-m // V7X_LANES) * V7X_LANES
    wb = w.astype(BF16)
    if m_pad != m:
        wb = jnp.pad(wb, ((0, 0), (0, m_pad - m)))
    tm = _row_tile(n)
    out = pl.pallas_call(
        _mm_kernel,
        grid=(n // tm,),
        in_specs=[pl.BlockSpec((tm, k), lambda i: (i, 0)),
                  pl.BlockSpec((k, m_pad), lambda i: (0, 0))],
        out_specs=pl.BlockSpec((tm, m_pad), lambda i: (i, 0)),
        out_shape=jax.ShapeDtypeStruct((n, m_pad), F32),
        compiler_params=pltpu.CompilerParams(dimension_semantics=("arbitrary",),
                                             vmem_limit_bytes=V7X_VMEM_LIMIT),
        name="matmul",
    )(x, wb)
    return out[:, :m] if m_pad != m else out


def matmul3(x, w):
    b, s, k = x.shape
    return matmul(x.reshape(b * s, k), w).reshape(b, s, w.shape[1])


def rmsnorm(x, g):
    xf = x.astype(F32)
    y = xf * lax.rsqrt(jnp.mean(xf * xf, axis=-1, keepdims=True) + EPS)
    return (y * g.astype(F32)).astype(x.dtype)


def rope(x, pos):
    half = x.shape[-1] // 2
    inv_freq = ROPE_THETA ** (-jnp.arange(half, dtype=F32) / half)
    ang = pos.astype(F32)[:, None] * inv_freq[None, :]
    cos = jnp.cos(ang)[None, :, None, :]
    sin = jnp.sin(ang)[None, :, None, :]
    xf = x.astype(F32)
    x1, x2 = xf[..., :half], xf[..., half:]
    return jnp.concatenate([x1 * cos - x2 * sin, x2 * cos + x1 * sin], axis=-1).astype(x.dtype)


def prompt_attention(q, k, v, fcum):
    bsz, seq, n_kv, n_grp, dk = q.shape
    dv = v.shape[-1]
    qb = math.gcd(seq, Q_BLOCK)
    scale = dk ** -0.5
    kpos = jnp.arange(seq)
    ft = None if fcum is None else jnp.transpose(fcum, (0, 2, 3, 1))

    def one_block(blk):
        start = blk * qb
        qblk = lax.dynamic_slice_in_dim(q, start, qb, axis=1)
        s = jnp.einsum('bqhgd,bkhd->bhgqk', qblk, k).astype(F32) * scale
        if ft is not None:
            fq = lax.dynamic_slice_in_dim(ft, start, qb, axis=3)
            s = s + fq[..., :, None] - ft[..., None, :]
        qpos = start + jnp.arange(qb)
        s = jnp.where(qpos[:, None] >= kpos[None, :], s, -jnp.inf)
        p = jax.nn.softmax(s, axis=-1)
        return jnp.einsum('bhgqk,bkhe->bqhge', p.astype(v.dtype), v)

    out = lax.map(one_block, jnp.arange(seq // qb))
    return jnp.moveaxis(out, 0, 1).reshape(bsz, seq, n_kv, n_grp, dv)


def _online_update(m, l, acc, s, v):
    m_new = jnp.maximum(m, s.max(-1))
    p = jnp.exp(s - m_new[..., None])
    corr = jnp.exp(m - m_new)
    acc = acc * corr[..., None] + jnp.einsum('bhgts,bshe->bhgte', p, v.astype(F32))
    return m_new, l * corr + p.sum(-1), acc


def decode_attention(q, k_new, v_new, lf_new, load_past, n_blocks):
    bd, t, n_kv, n_grp, dk = q.shape
    dv = v_new.shape[-1]
    scale = dk ** -0.5
    use_forget = lf_new is not None
    if use_forget:
        g_new = jnp.transpose(jnp.cumsum(lf_new, axis=1), (0, 2, 3, 1))
        g_q = g_new[..., :, None]

    def past_block(carry, j):
        m, l, acc = carry[:3]
        k, v, lf = load_past(j)
        s = jnp.einsum('bthgd,bshd->bhgts', q, k).astype(F32) * scale
        extra = ()
        if use_forget:
            suf = carry[3]
            tot = lf.sum(axis=1)
            s_past = suf[:, None] + tot[:, None] - jnp.cumsum(lf, axis=1)
            s = s + jnp.transpose(s_past, (0, 2, 3, 1))[:, :, :, None, :] + g_q
            extra = (suf + tot,)
        m, l, acc = _online_update(m, l, acc, s, v)
        return (m, l, acc) + extra, None

    init = (jnp.full((bd, n_kv, n_grp, t), -jnp.inf, F32), jnp.zeros((bd, n_kv, n_grp, t), F32),
            jnp.zeros((bd, n_kv, n_grp, t, dv), F32))
    if use_forget:
        init = init + (jnp.zeros((bd, n_kv, n_grp), F32),)
    carry, _ = lax.scan(past_block, init, jnp.arange(n_blocks), reverse=True)
    m, l, acc = carry[:3]
    s = jnp.einsum('bthgd,bshd->bhgts', q, k_new).astype(F32) * scale
    if use_forget:
        s = s + g_q - g_new[..., None, :]
    causal = jnp.tril(jnp.ones((t, t), bool))
    s = jnp.where(causal, s, -jnp.inf)
    m, l, acc = _online_update(m, l, acc, s, v_new)
    out = acc / l[..., None]
    return jnp.transpose(out, (0, 3, 1, 2, 4)).astype(v_new.dtype)


def mlstm_chunkwise(q, k, v, ig, lf, c0, n0, m0):
    bsz, seq, nh, dk = q.shape
    dv = v.shape[-1]
    ln = math.gcd(seq, MLSTM_CHUNK)
    nc = seq // ln
    q = q.astype(F32)
    k = k.astype(F32) * dk ** -0.5
    v = v.astype(F32)

    def chunks(a):
        return jnp.moveaxis(a.reshape((bsz, nc, ln) + a.shape[2:]), 1, 0)

    causal = jnp.tril(jnp.ones((ln, ln), bool))[None, :, :, None]

    def step(carry, inp):
        c, n, m = carry
        qc, kc, vc, ic, fc = inp
        b = jnp.cumsum(fc, axis=1)
        dmat = b[:, :, None, :] - b[:, None, :, :] + ic[:, None, :, :]
        dmat = jnp.where(causal, dmat, -jnp.inf)
        inter = b + m[:, None, :]
        mt = jnp.maximum(inter, dmat.max(axis=2))
        w = jnp.exp(dmat - mt[:, :, None, :])
        ei = jnp.exp(inter - mt)
        a = w * jnp.einsum('bthd,bshd->btsh', qc, kc)
        num = ei[..., None] * jnp.einsum('bhed,bthd->bthe', c, qc) + jnp.einsum('btsh,bshe->bthe', a, vc)
        den = ei * jnp.einsum('bhd,bthd->bth', n, qc) + a.sum(axis=2)
        h = num / jnp.maximum(jnp.abs(den), jnp.exp(-mt))[..., None]
        b_last = b[:, -1]
        wlog = b_last[:, None] - b + ic
        m_new = jnp.maximum(b_last + m, wlog.max(axis=1))
        ws = jnp.exp(wlog - m_new[:, None])
        decay = jnp.exp(b_last + m - m_new)
        c_new = decay[..., None, None] * c + jnp.einsum('bsh,bshe,bshd->bhed', ws, vc, kc)
        n_new = decay[..., None] * n + jnp.einsum('bsh,bshd->bhd', ws, kc)
        return (c_new, n_new, m_new), h

    init = (c0.astype(F32), n0.astype(F32), m0.astype(F32))
    (c1, n1, m1), hs = lax.scan(step, init, (chunks(q), chunks(k), chunks(v), chunks(ig), chunks(lf)))
    h = jnp.moveaxis(hs, 0, 1).reshape(bsz, seq, nh, dv)
    return h, c1, n1, m1


def _gelu(x):
    return 0.5 * x * (1.0 + lax.erf(x * (2.0 ** -0.5)))


S5_NCH = S5_GROUPS * S5_STATE
S5_SCAN_ROWS = 8


def _s5_scan_block(xr, xi, pr, pi, cr, ci, row):
    for dist in (1, 2, 4):
        ar = pr[dist - 1:dist]
        ai = pi[dist - 1:dist]
        sr = jnp.where(row >= dist, pltpu.roll(xr, dist, 0), 0.0)
        si = jnp.where(row >= dist, pltpu.roll(xi, dist, 0), 0.0)
        xr, xi = xr + ar * sr - ai * si, xi + ar * si + ai * sr
    return xr + pr * cr - pi * ci, xi + pr * ci + pi * cr


def _s5_readout(u, xr_ref, xi_ref, cre_ref, cim_ref, d_ref, wglu_ref, bglu_ref):
    y = (jnp.dot(xr_ref[...].astype(BF16), cre_ref[...], preferred_element_type=F32)
         - jnp.dot(xi_ref[...].astype(BF16), cim_ref[...], preferred_element_type=F32)
         + u * d_ref[...])
    y = _gelu(y)
    gate = jnp.dot(y.astype(BF16), wglu_ref[...], preferred_element_type=F32) + bglu_ref[...]
    return y * jax.nn.sigmoid(gate)


def _s5_project(u, bre_ref, bim_ref, xr_ref, xi_ref):
    xr_ref[...] = jnp.dot(u, bre_ref[...], preferred_element_type=F32, precision=lax.Precision.HIGHEST)
    xi_ref[...] = jnp.dot(u, bim_ref[...], preferred_element_type=F32, precision=lax.Precision.HIGHEST)


def _s5_seq_kernel(u_ref, bre_ref, bim_ref, cre_ref, cim_ref, d_ref, wglu_ref, bglu_ref, p_ref, x0_ref,
                   o_ref, st_ref, xr_ref, xi_ref, car_ref):
    ts = u_ref.shape[1]
    u = u_ref[0]
    _s5_project(u, bre_ref, bim_ref, xr_ref, xi_ref)

    @pl.when(pl.program_id(1) == 0)
    def _():
        car_ref[...] = x0_ref[0]

    pr = p_ref[0]
    pi = p_ref[1]
    row = lax.broadcasted_iota(jnp.int32, (S5_SCAN_ROWS, S5_NCH), 0)

    def blk(j, carry):
        cr, ci = carry
        r0 = pl.multiple_of(j * S5_SCAN_ROWS, S5_SCAN_ROWS)
        xr, xi = _s5_scan_block(xr_ref[pl.ds(r0, S5_SCAN_ROWS), :], xi_ref[pl.ds(r0, S5_SCAN_ROWS), :],
                                pr, pi, cr, ci, row)
        xr_ref[pl.ds(r0, S5_SCAN_ROWS), :] = xr
        xi_ref[pl.ds(r0, S5_SCAN_ROWS), :] = xi
        return xr[S5_SCAN_ROWS - 1:], xi[S5_SCAN_ROWS - 1:]

    cr, ci = lax.fori_loop(0, ts // S5_SCAN_ROWS, blk, (car_ref[0:1], car_ref[1:2]))
    car_ref[0:1] = cr
    car_ref[1:2] = ci
    st_ref[0] = car_ref[...]
    o_ref[0] = _s5_readout(u, xr_ref, xi_ref, cre_ref, cim_ref, d_ref, wglu_ref, bglu_ref)


def _s5_short_kernel(u_ref, bre_ref, bim_ref, cre_ref, cim_ref, d_ref, wglu_ref, bglu_ref, p_ref, x0_ref,
                     o_ref, st_ref, xr_ref, xi_ref):
    nseq = x0_ref.shape[0]
    u = u_ref[...]
    _s5_project(u, bre_ref, bim_ref, xr_ref, xi_ref)
    pr = p_ref[0]
    pi = p_ref[1]
    row = lax.broadcasted_iota(jnp.int32, (S5_SCAN_ROWS, S5_NCH), 0)

    def blk(j, carry):
        r0 = pl.multiple_of(j * S5_SCAN_ROWS, S5_SCAN_ROWS)
        x0 = x0_ref[j]
        xr, xi = _s5_scan_block(xr_ref[pl.ds(r0, S5_SCAN_ROWS), :], xi_ref[pl.ds(r0, S5_SCAN_ROWS), :],
                                pr, pi, x0[0:1], x0[1:2], row)
        xr_ref[pl.ds(r0, S5_SCAN_ROWS), :] = xr
        xi_ref[pl.ds(r0, S5_SCAN_ROWS), :] = xi
        st_ref[j, 0:1, :] = xr[S5_SCAN_ROWS - 1:]
        st_ref[j, 1:2, :] = xi[S5_SCAN_ROWS - 1:]
        return carry

    lax.fori_loop(0, nseq, blk, 0)
    o_ref[...] = _s5_readout(u, xr_ref, xi_ref, cre_ref, cim_ref, d_ref, wglu_ref, bglu_ref)


def _s5_tables(a_re, a_im, log_step, b_re, b_im, c_re, c_im):
    step = jnp.exp(log_step.astype(F32))[:, None]
    mag = jnp.exp(a_re * step)
    abr = mag * jnp.cos(a_im * step)
    abi = mag * jnp.sin(a_im * step)
    den = a_re * a_re + a_im * a_im
    cr = ((abr - 1.0) * a_re + abi * a_im) / den
    ci = (abi * a_re - (abr - 1.0) * a_im) / den
    bbr = cr[..., None] * b_re - ci[..., None] * b_im
    bbi = cr[..., None] * b_im + ci[..., None] * b_re
    eye = jnp.eye(S5_GROUPS, dtype=F32)
    bre = jnp.einsum('gpi,gh->gihp', bbr, eye).reshape(S5_CH, S5_NCH)
    bim = jnp.einsum('gpi,gh->gihp', bbi, eye).reshape(S5_CH, S5_NCH)
    cre = jnp.einsum('gop,gh->gpho', c_re, eye).reshape(S5_NCH, S5_CH).astype(BF16)
    cim = jnp.einsum('gop,gh->gpho', c_im, eye).reshape(S5_NCH, S5_CH).astype(BF16)
    ar = abr.reshape(1, S5_NCH)
    ai = abi.reshape(1, S5_NCH)
    pws = [(ar, ai)]
    for _ in range(S5_SCAN_ROWS - 1):
        qr, qi = pws[-1]
        pws.append((qr * ar - qi * ai, qr * ai + qi * ar))
    powers = jnp.stack([jnp.concatenate([p[0] for p in pws], axis=0),
                        jnp.concatenate([p[1] for p in pws], axis=0)], axis=0)
    return bre, bim, cre, cim, powers


def s5_layer(u, a_re, a_im, log_step, b_re, b_im, c_re, c_im, d, w_glu, b_glu, x0_re, x0_im):
    bsz, seq, _ = u.shape
    bre, bim, cre, cim, powers = _s5_tables(a_re, a_im, log_step, b_re, b_im, c_re, c_im)
    x0 = jnp.stack([x0_re.reshape(bsz, S5_NCH), x0_im.reshape(bsz, S5_NCH)], axis=1)
    d2 = d.astype(F32).reshape(1, S5_CH)
    bg = b_glu.astype(F32).reshape(1, S5_CH)
    wg = w_glu.astype(BF16)
    const = lambda *_: (0, 0)
    weight_specs = [pl.BlockSpec((S5_CH, S5_NCH), const), pl.BlockSpec((S5_CH, S5_NCH), const),
                    pl.BlockSpec((S5_NCH, S5_CH), const), pl.BlockSpec((S5_NCH, S5_CH), const),
                    pl.BlockSpec((1, S5_CH), const), pl.BlockSpec((S5_CH, S5_CH), const),
                    pl.BlockSpec((1, S5_CH), const),
                    pl.BlockSpec((2, S5_SCAN_ROWS, S5_NCH), lambda *_: (0, 0, 0))]
    weights = (bre, bim, cre, cim, d2, wg, bg, powers)
    if seq == S5_SCAN_ROWS:
        nseq = math.gcd(bsz, 64)
        rows = nseq * seq
        out, st = pl.pallas_call(
            _s5_short_kernel,
            grid=(bsz // nseq,),
            in_specs=[pl.BlockSpec((rows, S5_CH), lambda i: (i, 0))] + weight_specs
                     + [pl.BlockSpec((nseq, 2, S5_NCH), lambda i: (i, 0, 0))],
            out_specs=[pl.BlockSpec((rows, S5_CH), lambda i: (i, 0)),
                       pl.BlockSpec((nseq, 2, S5_NCH), lambda i: (i, 0, 0))],
            out_shape=[jax.ShapeDtypeStruct((bsz * seq, S5_CH), F32),
                       jax.ShapeDtypeStruct((bsz, 2, S5_NCH), F32)],
            scratch_shapes=[pltpu.VMEM((rows, S5_NCH), F32), pltpu.VMEM((rows, S5_NCH), F32)],
            compiler_params=pltpu.CompilerParams(dimension_semantics=("arbitrary",),
                                                 vmem_limit_bytes=V7X_VMEM_LIMIT),
            name="s5_short",
        )(u.reshape(bsz * seq, S5_CH), *weights, x0)
        out = out.reshape(bsz, seq, S5_CH)
    else:
        ts = math.gcd(seq, 512)
        out, st = pl.pallas_call(
            _s5_seq_kernel,
            grid=(bsz, seq // ts),
            in_specs=[pl.BlockSpec((1, ts, S5_CH), lambda b, s: (b, s, 0))] + weight_specs
                     + [pl.BlockSpec((1, 2, S5_NCH), lambda b, s: (b, 0, 0))],
            out_specs=[pl.BlockSpec((1, ts, S5_CH), lambda b, s: (b, s, 0)),
                       pl.BlockSpec((1, 2, S5_NCH), lambda b, s: (b, 0, 0))],
            out_shape=[jax.ShapeDtypeStruct((bsz, seq, S5_CH), F32),
                       jax.ShapeDtypeStruct((bsz, 2, S5_NCH), F32)],
            scratch_shapes=[pltpu.VMEM((ts, S5_NCH), F32), pltpu.VMEM((ts, S5_NCH), F32),
                            pltpu.VMEM((2, S5_NCH), F32)],
            compiler_params=pltpu.CompilerParams(dimension_semantics=("arbitrary", "arbitrary"),
                                                 vmem_limit_bytes=V7X_VMEM_LIMIT),
            name="s5_seq",
        )(u, *weights, x0)
    s_re = st[:, 0].reshape(bsz, S5_GROUPS, S5_STATE)
    s_im = st[:, 1].reshape(bsz, S5_GROUPS, S5_STATE)
    return out, s_re, s_im


def mla_keys(lat, krope, w_uk, w_uv, g_kn):
    k_nope = rmsnorm(jnp.einsum('bsr,rhd->bshd', lat, w_uk), g_kn)
    v = jnp.einsum('bsr,rhd->bshd', lat, w_uv)
    kr = jnp.broadcast_to(krope[:, :, None, :], k_nope.shape[:3] + (MLA_ROPE,)).astype(k_nope.dtype)
    return jnp.concatenate([k_nope, kr], axis=-1), v


PEER_HALF = PEER_KEY_DIM // 2
PEER_QW = PEER_HEADS * PEER_KEY_DIM
PEER_ROUTE_TOKENS = 256
PEER_TOKENS = 512
PEER_KEY_ROWS = 8
PEER_EXPERT_BLOCK = PEER_KEY_ROWS * PEER_NKEYS
PEER_ROW_BLOCK = 32


def _extract_top(x, rows_f):
    slot = lax.broadcasted_iota(jnp.int32, (PEER_TOPK, x.shape[1]), 0)

    def body(r, carry):
        x, sv = carry
        m = jnp.max(x, axis=0, keepdims=True)
        first = jnp.min(jnp.where(x == m, rows_f, float(x.shape[0])), axis=0, keepdims=True)
        x = jnp.where(rows_f == first, -jnp.inf, x)
        sv = jnp.where(slot == r, m, sv)
        return x, sv

    _, sv = lax.fori_loop(0, PEER_TOPK, body, (x, jnp.zeros((PEER_TOPK, x.shape[1]), F32)))
    return sv


def _peer_route_kernel(h_ref, g_ref, wqt_ref, sk_ref, xb_ref, s0_ref, scl_ref, s1_ref, e1_ref, tau_ref,
                       st_ref, cand_ref):
    tr = h_ref.shape[0]
    hf = h_ref[...]
    c = hf * lax.rsqrt(jnp.mean(hf * hf, axis=-1, keepdims=True) + EPS) * g_ref[...]
    cb = c.astype(BF16)
    xb_ref[...] = cb
    qt = lax.dot_general(wqt_ref[...], cb, (((1,), (1,)), ((), ())), preferred_element_type=F32)
    for hc in range(2 * PEER_HEADS):
        st_ref[hc] = jnp.dot(sk_ref[hc], qt[hc * PEER_HALF:(hc + 1) * PEER_HALF].astype(BF16),
                             preferred_element_type=F32)

    rows_key = lax.broadcasted_iota(jnp.int32, (PEER_NKEYS, V7X_LANES), 0).astype(F32)
    rows_cand = lax.broadcasted_iota(jnp.int32, (PEER_TOPK * PEER_TOPK, V7X_LANES), 0).astype(F32)

    def head(h, carry):
        for lg in range(tr // V7X_LANES):
            lanes = slice(lg * V7X_LANES, (lg + 1) * V7X_LANES)
            s0 = st_ref[2 * h, :, lanes]
            s1 = st_ref[2 * h + 1, :, lanes]
            sv0 = _extract_top(s0, rows_key)
            sv1 = _extract_top(s1, rows_key)
            for a in range(PEER_TOPK):
                cand_ref[a * PEER_TOPK:(a + 1) * PEER_TOPK, :] = sv0[a:a + 1] + sv1
            best = _extract_top(cand_ref[...], rows_cand)
            top = best[0:1]
            z = jnp.sum(jnp.exp(best - top), axis=0, keepdims=True)
            s0_ref[h, :, lanes] = s0
            scl_ref[h, :, lanes] = jnp.exp(s0 - sv0[0:1]) / z
            s1_ref[h, :, lanes] = s1
            e1_ref[h, :, lanes] = jnp.exp(s1 - sv1[0:1])
            tau_ref[h, :, lanes] = best[PEER_TOPK - 1:PEER_TOPK]
        return carry

    lax.fori_loop(0, PEER_HEADS, head, 0)


def _peer_expert_kernel(x_ref, u_ref, vt_ref, s0_ref, scl_ref, s1_ref, e1_ref, tau_ref, o_ref,
                        acc_ref, a_ref, s_ref):
    ei = pl.program_id(1)
    tt = x_ref.shape[0]
    key_rows = u_ref.shape[0] // PEER_NKEYS

    @pl.when(ei == 0)
    def _():
        acc_ref[...] = jnp.zeros_like(acc_ref)

    s_ref[...] = lax.dot_general(u_ref[...], x_ref[...], (((1,), (1,)), ((), ())), preferred_element_type=F32)

    assert key_rows == PEER_KEY_ROWS
    first_keys = pl.ds(pl.multiple_of(ei * PEER_KEY_ROWS, PEER_KEY_ROWS), PEER_KEY_ROWS)
    for lg in range(tt // V7X_LANES):
        lanes = slice(lg * V7X_LANES, (lg + 1) * V7X_LANES)
        s0rows = [s0_ref[h, first_keys, lanes] for h in range(PEER_HEADS)]
        sclrows = [scl_ref[h, first_keys, lanes] for h in range(PEER_HEADS)]
        taurow = [tau_ref[h, :, lanes] for h in range(PEER_HEADS)]
        for r in range(key_rows):
            s0row = [v[r:r + 1] for v in s0rows]
            sclrow = [v[r:r + 1] for v in sclrows]
            for jb in range(PEER_NKEYS // PEER_ROW_BLOCK):
                rs = slice(jb * PEER_ROW_BLOCK, (jb + 1) * PEER_ROW_BLOCK)
                es = slice(r * PEER_NKEYS + jb * PEER_ROW_BLOCK, r * PEER_NKEYS + (jb + 1) * PEER_ROW_BLOCK)
                gate = jnp.zeros((PEER_ROW_BLOCK, V7X_LANES), F32)
                for h in range(PEER_HEADS):
                    pair = s1_ref[h, rs, lanes] + s0row[h]
                    gate = gate + jnp.where(pair >= taurow[h], e1_ref[h, rs, lanes], 0.0) * sclrow[h]
                a_ref[es, lanes] = (gate * _gelu(s_ref[es, lanes])).astype(BF16)

    acc_ref[...] += jnp.dot(vt_ref[...], a_ref[...], preferred_element_type=F32)

    @pl.when(ei == pl.num_programs(1) - 1)
    def _():
        o_ref[...] = acc_ref[...].T


def peer_ffn(hres, g_ffn, w_q, subkeys, u_tab, v_tab):
    bsz, seq, dm = hres.shape
    ntok = bsz * seq
    nexp = u_tab.shape[0]
    tr = math.gcd(ntok, PEER_ROUTE_TOKENS)
    tt = math.gcd(ntok, PEER_TOKENS)
    wqt = w_q.T.astype(BF16)
    sk = subkeys.reshape(2 * PEER_HEADS, PEER_NKEYS, PEER_HALF).astype(BF16)
    ub = u_tab.astype(BF16)
    vt = v_tab.T.astype(BF16)
    key_shape = (PEER_HEADS, PEER_NKEYS, ntok)
    key_block = lambda t: pl.BlockSpec((PEER_HEADS, PEER_NKEYS, t), lambda i, *_: (0, 0, i))
    tau_block = lambda t: pl.BlockSpec((PEER_HEADS, 1, t), lambda i, *_: (0, 0, i))
    xb, s0t, sclt, s1t, e1t, taut = pl.pallas_call(
        _peer_route_kernel,
        grid=(ntok // tr,),
        in_specs=[pl.BlockSpec((tr, dm), lambda i: (i, 0)),
                  pl.BlockSpec((1, dm), lambda i: (0, 0)),
                  pl.BlockSpec((PEER_QW, dm), lambda i: (0, 0)),
                  pl.BlockSpec((2 * PEER_HEADS, PEER_NKEYS, PEER_HALF), lambda i: (0, 0, 0))],
        out_specs=[pl.BlockSpec((tr, dm), lambda i: (i, 0)),
                   key_block(tr), key_block(tr), key_block(tr), key_block(tr), tau_block(tr)],
        out_shape=[jax.ShapeDtypeStruct((ntok, dm), BF16)] + [jax.ShapeDtypeStruct(key_shape, F32)] * 4
                  + [jax.ShapeDtypeStruct((PEER_HEADS, 1, ntok), F32)],
        scratch_shapes=[pltpu.VMEM((2 * PEER_HEADS, PEER_NKEYS, tr), F32),
                        pltpu.VMEM((PEER_TOPK * PEER_TOPK, V7X_LANES), F32)],
        compiler_params=pltpu.CompilerParams(dimension_semantics=("arbitrary",),
                                             vmem_limit_bytes=V7X_VMEM_LIMIT),
        name="peer_route",
    )(hres.reshape(ntok, dm), g_ffn.astype(F32).reshape(1, dm), wqt, sk)

    eb = PEER_EXPERT_BLOCK
    out = pl.pallas_call(
        _peer_expert_kernel,
        grid=(ntok // tt, nexp // eb),
        in_specs=[pl.BlockSpec((tt, dm), lambda t, e: (t, 0)),
                  pl.BlockSpec((eb, dm), lambda t, e: (e, 0)),
                  pl.BlockSpec((dm, eb), lambda t, e: (0, e)),
                  key_block(tt), key_block(tt), key_block(tt), key_block(tt), tau_block(tt)],
        out_specs=pl.BlockSpec((tt, dm), lambda t, e: (t, 0)),
        out_shape=jax.ShapeDtypeStruct((ntok, dm), F32),
        scratch_shapes=[pltpu.VMEM((dm, tt), F32), pltpu.VMEM((eb, tt), BF16), pltpu.VMEM((eb, tt), F32)],
        compiler_params=pltpu.CompilerParams(dimension_semantics=("arbitrary", "arbitrary"),
                                             vmem_limit_bytes=V7X_VMEM_LIMIT),
        name="peer_experts",
    )(xb, ub, vt, s0t, sclt, s1t, e1t, taut)
    return out.reshape(bsz, seq, dm)


def mixing_sublayer(a, pos, lp, past):
    bsz, seq, _ = a.shape
    z = matmul3(a, lp['w_in'])
    points = [int(t) for t in np.cumsum(SPLIT_SIZES)[:-1]]
    (fq, fk, fv, ff, mq, mk, mv, mi, mf, mo, su, cq, ckv, kr) = jnp.split(z, points, axis=-1)

    q_f = rmsnorm(fq.reshape(bsz, seq, FOX_KV_HEADS, FOX_GROUP, HEAD_DIM), lp['g_fox_q'])
    k_f = rmsnorm(fk.reshape(bsz, seq, FOX_KV_HEADS, HEAD_DIM), lp['g_fox_k'])
    v_f = fv.reshape(bsz, seq, FOX_KV_HEADS, HEAD_DIM)
    lf_f = jax.nn.log_sigmoid(ff.astype(F32) + lp['b_fox_f'].astype(F32))
    lf_g = lf_f.reshape(bsz, seq, FOX_KV_HEADS, FOX_GROUP)

    c_q = rmsnorm(cq, lp['g_mla_cq'])
    q_m = jnp.einsum('bsr,rhd->bshd', c_q, lp['w_mla_uq'])
    q_m = jnp.concatenate([rmsnorm(q_m[..., :MLA_NOPE], lp['g_mla_qn']),
                           rope(rmsnorm(q_m[..., MLA_NOPE:], lp['g_mla_qr']), pos)], axis=-1)[:, :, :, None, :]
    lat = rmsnorm(ckv, lp['g_mla_ckv'])
    krope = rope(rmsnorm(kr, lp['g_mla_kr'])[:, :, None, :], pos)[:, :, 0, :]
    k_m, v_m = mla_keys(lat, krope, lp['w_mla_uk'], lp['w_mla_uv'], lp['g_mla_kn'])

    if past is None:
        o_f = prompt_attention(q_f, k_f, v_f, jnp.cumsum(lf_g, axis=1))
        o_m = prompt_attention(q_m, k_m, v_m, None)
        c0 = jnp.zeros((bsz, MLSTM_HEADS, HEAD_DIM, HEAD_DIM), F32)
        n0 = jnp.zeros((bsz, MLSTM_HEADS, HEAD_DIM), F32)
        m0 = jnp.zeros((bsz, MLSTM_HEADS), F32)
        x0_re = jnp.zeros((bsz, S5_GROUPS, S5_STATE), F32)
        x0_im = jnp.zeros((bsz, S5_GROUPS, S5_STATE), F32)
    else:
        o_f = decode_attention(q_f, k_f, v_f, lf_g, past['load_fox'], past['n_blocks'])
        o_m = decode_attention(q_m, k_m, v_m, None, past['load_mla'], past['n_blocks'])
        c0, n0, m0, x0_re, x0_im = past['rec']

    ig = mi.astype(F32) + lp['b_mlstm_i'].astype(F32)
    lf_l = jax.nn.log_sigmoid(mf.astype(F32) + lp['b_mlstm_f'].astype(F32))
    h_l, c1, n1, m1 = mlstm_chunkwise(mq.reshape(bsz, seq, MLSTM_HEADS, HEAD_DIM),
                                      mk.reshape(bsz, seq, MLSTM_HEADS, HEAD_DIM),
                                      mv.reshape(bsz, seq, MLSTM_HEADS, HEAD_DIM), ig, lf_l, c0, n0, m0)
    o_l = rmsnorm(h_l, lp['g_mlstm_h']).astype(a.dtype).reshape(bsz, seq, MLSTM_W) * jax.nn.sigmoid(mo)

    o_s, s_re, s_im = s5_layer(su, lp['s5_a_re'], lp['s5_a_im'], lp['s5_log_step'], lp['s5_b_re'], lp['s5_b_im'],
                               lp['s5_c_re'], lp['s5_c_im'], lp['s5_d'], lp['w_glu'], lp['b_glu'], x0_re, x0_im)

    mix = jnp.concatenate([o_f.reshape(bsz, seq, -1), o_l, o_s, o_m.reshape(bsz, seq, -1)], axis=-1)
    out = matmul3(mix, lp['w_out'])
    return out, (k_f, v_f, lf_f, lat, krope, c1, n1, m1, s_re, s_im)


def run_trunk(x, p, pos, layer_params, pasts):
    h = x
    states = []
    for i in range(DEPTH):
        lp = layer_params[i]
        mix, st = mixing_sublayer(rmsnorm(h, lp['g_mix']), pos, lp, pasts[i])
        h = h + mix
        h = h + peer_ffn(h, lp['g_ffn'], lp['w_peer_q'], lp['peer_subkeys'], lp['peer_u'], lp['peer_v'])
        gate = jax.nn.sigmoid(matmul3(rmsnorm(h, lp['g_ple']), lp['w_ple_gate']))
        h = h + matmul3(p[i], lp['w_ple']) * gate
        states.append(st)
    stacked = [jnp.stack([st[j] for st in states], axis=0) for j in range(len(states[0]))]
    return h, stacked


def kernel(x_prompt, x_sample, cache_fox_k, cache_fox_v, cache_fox_logf, cache_mla_latent, cache_mla_krope,
           state_mlstm_C, state_mlstm_n, state_mlstm_m, state_s5_re, state_s5_im, page_table, p_prompt, p_sample,
           g_mix, w_in, g_fox_q, g_fox_k, b_fox_f, b_mlstm_i, b_mlstm_f, g_mlstm_h,
           s5_a_re, s5_a_im, s5_log_step, s5_b_re, s5_b_im, s5_c_re, s5_c_im, s5_d, w_glu, b_glu,
           g_mla_cq, w_mla_uq, g_mla_qn, g_mla_qr, g_mla_ckv, g_mla_kr, w_mla_uk, w_mla_uv, g_mla_kn,
           w_out, g_ffn, w_peer_q, peer_subkeys, peer_u, peer_v, g_ple, w_ple_gate, w_ple):
    names = ('g_mix', 'w_in', 'g_fox_q', 'g_fox_k', 'b_fox_f', 'b_mlstm_i', 'b_mlstm_f', 'g_mlstm_h',
             's5_a_re', 's5_a_im', 's5_log_step', 's5_b_re', 's5_b_im', 's5_c_re', 's5_c_im', 's5_d', 'w_glu',
             'b_glu', 'g_mla_cq', 'w_mla_uq', 'g_mla_qn', 'g_mla_qr', 'g_mla_ckv', 'g_mla_kr', 'w_mla_uk',
             'w_mla_uv', 'g_mla_kn', 'w_out', 'g_ffn', 'w_peer_q', 'peer_subkeys', 'peer_u', 'peer_v', 'g_ple',
             'w_ple_gate', 'w_ple')
    vals = (g_mix, w_in, g_fox_q, g_fox_k, b_fox_f, b_mlstm_i, b_mlstm_f, g_mlstm_h,
            s5_a_re, s5_a_im, s5_log_step, s5_b_re, s5_b_im, s5_c_re, s5_c_im, s5_d, w_glu,
            b_glu, g_mla_cq, w_mla_uq, g_mla_qn, g_mla_qr, g_mla_ckv, g_mla_kr, w_mla_uk,
            w_mla_uv, g_mla_kn, w_out, g_ffn, w_peer_q, peer_subkeys, peer_u, peer_v, g_ple,
            w_ple_gate, w_ple)
    layer_params = [{n: v[i] for n, v in zip(names, vals)} for i in range(DEPTH)]

    pos_p = jnp.arange(x_prompt.shape[1], dtype=jnp.int32)
    y_prompt, st_p = run_trunk(x_prompt, p_prompt, pos_p, layer_params, [None] * DEPTH)

    n_pages = page_table.shape[1]
    past_len = n_pages * PAGE_SIZE
    bp = math.gcd(n_pages, MAX_KEY_BLOCK_PAGES)
    n_blocks = n_pages // bp
    rows = bp * PAGE_SIZE
    db = x_sample.shape[0]

    def make_past(i):
        def pages_of(j):
            return lax.dynamic_slice_in_dim(page_table, j * bp, bp, axis=1)

        def load_fox(j):
            pg = pages_of(j)
            k = cache_fox_k[i, pg].reshape(db, rows, FOX_KV_HEADS, HEAD_DIM)
            v = cache_fox_v[i, pg].reshape(db, rows, FOX_KV_HEADS, HEAD_DIM)
            lf = cache_fox_logf[i, pg].reshape(db, rows, FOX_KV_HEADS, FOX_GROUP).astype(F32)
            return k, v, lf

        def load_mla(j):
            pg = pages_of(j)
            lat = cache_mla_latent[i, pg].reshape(db, rows, MLA_KV_RANK)
            kr = cache_mla_krope[i, pg].reshape(db, rows, MLA_ROPE)
            k, v = mla_keys(lat, kr, w_mla_uk[i], w_mla_uv[i], g_mla_kn[i])
            return k, v, None

        return {'load_fox': load_fox, 'load_mla': load_mla, 'n_blocks': n_blocks,
                'rec': (state_mlstm_C[i], state_mlstm_n[i], state_mlstm_m[i], state_s5_re[i], state_s5_im[i])}

    pasts = [make_past(i) for i in range(DEPTH)]
    pos_s = past_len + jnp.arange(x_sample.shape[1], dtype=jnp.int32)
    y_sample, st_s = run_trunk(x_sample, p_sample, pos_s, layer_params, pasts)

    (pk, pv, plf, plat, pkr, pc, pn, pm, pre, pim) = st_p
    (sk, sv, slf, slat, skr, sc, sn, sm, sre, sim) = st_s
    return (y_prompt, y_sample, pk, pv, plf, plat, pkr, pc, pn, pm, pre, pim,
            sk, sv, slf, slat, skr, sc, sn, sm, sre, sim)
```

```python
import functools
import math

import jax
import jax.numpy as jnp
import numpy as np
from jax import lax
from jax.experimental import pallas as pl
from jax.experimental.pallas import tpu as pltpu

D_MODEL = 1024
DEPTH = 2
PAGE_SIZE = 128
HEAD_DIM = 64
GROUP_WIDTH = D_MODEL // 4
FOX_HEADS = GROUP_WIDTH // HEAD_DIM
FOX_KV_HEADS = FOX_HEADS // 2
FOX_GROUP = FOX_HEADS // FOX_KV_HEADS
MLSTM_HEADS = GROUP_WIDTH // HEAD_DIM
MLSTM_W = MLSTM_HEADS * HEAD_DIM
MLSTM_CHUNK = 64
S5_CH = GROUP_WIDTH
S5_GROUP_SIZE = 16
S5_GROUPS = S5_CH // S5_GROUP_SIZE
S5_STATE = 64
MLA_HEADS = GROUP_WIDTH // HEAD_DIM
MLA_NOPE = HEAD_DIM
MLA_ROPE = HEAD_DIM // 2
MLA_V = HEAD_DIM
MLA_Q_RANK = D_MODEL // 8
MLA_KV_RANK = D_MODEL // 8
ROPE_THETA = 10000.0
PEER_HEADS = 8
PEER_NKEYS = 128
PEER_KEY_DIM = 256
PEER_TOPK = 16
PEER_TOKEN_BLOCK = 256
Q_BLOCK = 128
MAX_KEY_BLOCK_PAGES = 8
EPS = 1e-6
F32 = jnp.float32
BF16 = jnp.bfloat16

SPLIT_SIZES = (FOX_HEADS * HEAD_DIM, FOX_KV_HEADS * HEAD_DIM, FOX_KV_HEADS * HEAD_DIM, FOX_HEADS,
               MLSTM_W, MLSTM_W, MLSTM_W, MLSTM_HEADS, MLSTM_HEADS, MLSTM_W,
               S5_CH,
               MLA_Q_RANK, MLA_KV_RANK, MLA_ROPE)

V7X_LANES = 128
V7X_VMEM_LIMIT = 48 * 1024 * 1024


def _mm_kernel(x_ref, w_ref, o_ref):
    o_ref[...] = jnp.dot(x_ref[...].astype(BF16), w_ref[...], preferred_element_type=F32)


def _row_tile(n):
    for t in (512, 256, 128, 64, 32, 16, 8):
        if n % t == 0:
            return t
    raise ValueError(f"row count {n} is not a multiple of 8")


def matmul(x, w):
    n, k = x.shape
    m = w.shape[1]
    m_pad = -(-m // V7X_LANES) * V7X_LANES
    wb = w.astype(BF16)
    if m_pad != m:
        wb = jnp.pad(wb, ((0, 0), (0, m_pad - m)))
    tm = _row_tile(n)
    out = pl.pallas_call(
        _mm_kernel,
        grid=(n // tm,),
        in_specs=[pl.BlockSpec((tm, k), lambda i: (i, 0)),
                  pl.BlockSpec((k, m_pad), lambda i: (0, 0))],
        out_specs=pl.BlockSpec((tm, m_pad), lambda i: (i, 0)),
        out_shape=jax.ShapeDtypeStruct((n, m_pad), F32),
        compiler_params=pltpu.CompilerParams(dimension_semantics=("arbitrary",),
                                             vmem_limit_bytes=V7X_VMEM_LIMIT),
        name="matmul",
    )(x, wb)
    return out[:, :m] if m_pad != m else out


def matmul3(x, w):
    b, s, k = x.shape
    return matmul(x.reshape(b * s, k), w).reshape(b, s, w.shape[1])


def rmsnorm(x, g):
    xf = x.astype(F32)
    y = xf * lax.rsqrt(jnp.mean(xf * xf, axis=-1, keepdims=True) + EPS)
    return (y * g.astype(F32)).astype(x.dtype)


def rope(x, pos):
    half = x.shape[-1] // 2
    inv_freq = ROPE_THETA ** (-jnp.arange(half, dtype=F32) / half)
    ang = pos.astype(F32)[:, None] * inv_freq[None, :]
    cos = jnp.cos(ang)[None, :, None, :]
    sin = jnp.sin(ang)[None, :, None, :]
    xf = x.astype(F32)
    x1, x2 = xf[..., :half], xf[..., half:]
    return jnp.concatenate([x1 * cos - x2 * sin, x2 * cos + x1 * sin], axis=-1).astype(x.dtype)


ATTN_BLOCK = 512


def _flash_kernel(q_ref, k_ref, v_ref, fq_ref, fk_ref, o_ref, m_ref, l_ref, acc_ref, *, scale, use_forget):
    qi = pl.program_id(2)
    ki = pl.program_id(3)

    @pl.when(ki == 0)
    def _():
        m_ref[...] = jnp.full_like(m_ref, -jnp.inf)
        l_ref[...] = jnp.zeros_like(l_ref)
        acc_ref[...] = jnp.zeros_like(acc_ref)

    def step(diagonal):
        s = lax.dot_general(q_ref[0, 0], k_ref[0, 0], (((1,), (1,)), ((), ())),
                            preferred_element_type=F32) * scale
        if use_forget:
            s = s + fq_ref[0, 0] - fk_ref[0, 0]
        if diagonal:
            row = lax.broadcasted_iota(jnp.int32, s.shape, 0)
            col = lax.broadcasted_iota(jnp.int32, s.shape, 1)
            s = jnp.where(row >= col, s, -jnp.inf)
        m_prev = m_ref[...]
        m_new = jnp.maximum(m_prev, jnp.max(s, axis=-1, keepdims=True))
        p = jnp.exp(s - m_new)
        corr = jnp.exp(m_prev - m_new)
        l_ref[...] = corr * l_ref[...] + jnp.sum(p, axis=-1, keepdims=True)
        acc_ref[...] = corr * acc_ref[...] + jnp.dot(p.astype(BF16), v_ref[0, 0], preferred_element_type=F32)
        m_ref[...] = m_new

    @pl.when(ki < qi)
    def _():
        step(False)

    @pl.when(ki == qi)
    def _():
        step(True)
        o_ref[0, 0] = acc_ref[...] / l_ref[...]


def prompt_attention(q, k, v, fcum):
    bsz, seq, n_kv, n_grp, dk = q.shape
    dv = v.shape[-1]
    nh = n_kv * n_grp
    tb = math.gcd(seq, ATTN_BLOCK)
    nb = seq // tb
    qh = jnp.transpose(q.reshape(bsz, seq, nh, dk), (0, 2, 1, 3)).astype(BF16)
    kh = jnp.transpose(k, (0, 2, 1, 3)).astype(BF16)
    vh = jnp.transpose(v, (0, 2, 1, 3)).astype(BF16)
    use_forget = fcum is not None
    if use_forget:
        ft = jnp.transpose(fcum.reshape(bsz, seq, nh), (0, 2, 1)).astype(F32)
    else:
        ft = jnp.zeros((bsz, nh, seq), F32)
    fq = ft[..., None]
    fk = ft[:, :, None, :]
    out = pl.pallas_call(
        functools.partial(_flash_kernel, scale=dk ** -0.5, use_forget=use_forget),
        grid=(bsz, nh, nb, nb),
        in_specs=[pl.BlockSpec((1, 1, tb, dk), lambda b, h, i, j: (b, h, i, 0)),
                  pl.BlockSpec((1, 1, tb, dk), lambda b, h, i, j: (b, h // n_grp, jnp.minimum(i, j), 0)),
                  pl.BlockSpec((1, 1, tb, dv), lambda b, h, i, j: (b, h // n_grp, jnp.minimum(i, j), 0)),
                  pl.BlockSpec((1, 1, tb, 1), lambda b, h, i, j: (b, h, i, 0)),
                  pl.BlockSpec((1, 1, 1, tb), lambda b, h, i, j: (b, h, 0, jnp.minimum(i, j)))],
        out_specs=pl.BlockSpec((1, 1, tb, dv), lambda b, h, i, j: (b, h, i, 0)),
        out_shape=jax.ShapeDtypeStruct((bsz, nh, seq, dv), F32),
        scratch_shapes=[pltpu.VMEM((tb, 1), F32), pltpu.VMEM((tb, 1), F32), pltpu.VMEM((tb, dv), F32)],
        compiler_params=pltpu.CompilerParams(
            dimension_semantics=("arbitrary", "arbitrary", "arbitrary", "arbitrary"),
            vmem_limit_bytes=V7X_VMEM_LIMIT),
        name="flash_attention",
    )(qh, kh, vh, fq, fk)
    return jnp.transpose(out, (0, 2, 1, 3)).reshape(bsz, seq, n_kv, n_grp, dv)


def _online_update(m, l, acc, s, v):
    m_new = jnp.maximum(m, s.max(-1))
    p = jnp.exp(s - m_new[..., None])
    corr = jnp.exp(m - m_new)
    acc = acc * corr[..., None] + jnp.einsum('bhgts,bshe->bhgte', p, v.astype(F32))
    return m_new, l * corr + p.sum(-1), acc


def decode_attention(q, k_new, v_new, lf_new, load_past, n_blocks):
    bd, t, n_kv, n_grp, dk = q.shape
    dv = v_new.shape[-1]
    scale = dk ** -0.5
    use_forget = lf_new is not None
    if use_forget:
        g_new = jnp.transpose(jnp.cumsum(lf_new, axis=1), (0, 2, 3, 1))
        g_q = g_new[..., :, None]

    def past_block(carry, j):
        m, l, acc = carry[:3]
        k, v, lf = load_past(j)
        s = jnp.einsum('bthgd,bshd->bhgts', q, k).astype(F32) * scale
        extra = ()
        if use_forget:
            suf = carry[3]
            tot = lf.sum(axis=1)
            s_past = suf[:, None] + tot[:, None] - jnp.cumsum(lf, axis=1)
            s = s + jnp.transpose(s_past, (0, 2, 3, 1))[:, :, :, None, :] + g_q
            extra = (suf + tot,)
        m, l, acc = _online_update(m, l, acc, s, v)
        return (m, l, acc) + extra, None

    init = (jnp.full((bd, n_kv, n_grp, t), -jnp.inf, F32), jnp.zeros((bd, n_kv, n_grp, t), F32),
            jnp.zeros((bd, n_kv, n_grp, t, dv), F32))
    if use_forget:
        init = init + (jnp.zeros((bd, n_kv, n_grp), F32),)
    carry, _ = lax.scan(past_block, init, jnp.arange(n_blocks), reverse=True)
    m, l, acc = carry[:3]
    s = jnp.einsum('bthgd,bshd->bhgts', q, k_new).astype(F32) * scale
    if use_forget:
        s = s + g_q - g_new[..., None, :]
    causal = jnp.tril(jnp.ones((t, t), bool))
    s = jnp.where(causal, s, -jnp.inf)
    m, l, acc = _online_update(m, l, acc, s, v_new)
    out = acc / l[..., None]
    return jnp.transpose(out, (0, 3, 1, 2, 4)).astype(v_new.dtype)


def mlstm_chunkwise(q, k, v, ig, lf, c0, n0, m0):
    bsz, seq, nh, dk = q.shape
    dv = v.shape[-1]
    ln = math.gcd(seq, MLSTM_CHUNK)
    nc = seq // ln
    q = q.astype(F32)
    k = k.astype(F32) * dk ** -0.5
    v = v.astype(F32)

    def chunks(a):
        return jnp.moveaxis(a.reshape((bsz, nc, ln) + a.shape[2:]), 1, 0)

    causal = jnp.tril(jnp.ones((ln, ln), bool))[None, :, :, None]

    def step(carry, inp):
        c, n, m = carry
        qc, kc, vc, ic, fc = inp
        b = jnp.cumsum(fc, axis=1)
        dmat = b[:, :, None, :] - b[:, None, :, :] + ic[:, None, :, :]
        dmat = jnp.where(causal, dmat, -jnp.inf)
        inter = b + m[:, None, :]
        mt = jnp.maximum(inter, dmat.max(axis=2))
        w = jnp.exp(dmat - mt[:, :, None, :])
        ei = jnp.exp(inter - mt)
        a = w * jnp.einsum('bthd,bshd->btsh', qc, kc)
        num = ei[..., None] * jnp.einsum('bhed,bthd->bthe', c, qc) + jnp.einsum('btsh,bshe->bthe', a, vc)
        den = ei * jnp.einsum('bhd,bthd->bth', n, qc) + a.sum(axis=2)
        h = num / jnp.maximum(jnp.abs(den), jnp.exp(-mt))[..., None]
        b_last = b[:, -1]
        wlog = b_last[:, None] - b + ic
        m_new = jnp.maximum(b_last + m, wlog.max(axis=1))
        ws = jnp.exp(wlog - m_new[:, None])
        decay = jnp.exp(b_last + m - m_new)
        c_new = decay[..., None, None] * c + jnp.einsum('bsh,bshe,bshd->bhed', ws, vc, kc)
        n_new = decay[..., None] * n + jnp.einsum('bsh,bshd->bhd', ws, kc)
        return (c_new, n_new, m_new), h

    init = (c0.astype(F32), n0.astype(F32), m0.astype(F32))
    (c1, n1, m1), hs = lax.scan(step, init, (chunks(q), chunks(k), chunks(v), chunks(ig), chunks(lf)))
    h = jnp.moveaxis(hs, 0, 1).reshape(bsz, seq, nh, dv)
    return h, c1, n1, m1


def _gelu(x):
    return 0.5 * x * (1.0 + lax.erf(x * (2.0 ** -0.5)))


S5_NCH = S5_GROUPS * S5_STATE
S5_SCAN_ROWS = 8


def _s5_scan_block(xr, xi, pr, pi, cr, ci, row):
    for dist in (1, 2, 4):
        ar = pr[dist - 1:dist]
        ai = pi[dist - 1:dist]
        sr = jnp.where(row >= dist, pltpu.roll(xr, dist, 0), 0.0)
        si = jnp.where(row >= dist, pltpu.roll(xi, dist, 0), 0.0)
        xr, xi = xr + ar * sr - ai * si, xi + ar * si + ai * sr
    return xr + pr * cr - pi * ci, xi + pr * ci + pi * cr


def _s5_readout(u, xr_ref, xi_ref, cre_ref, cim_ref, d_ref, wglu_ref, bglu_ref):
    y = (jnp.dot(xr_ref[...].astype(BF16), cre_ref[...], preferred_element_type=F32)
         - jnp.dot(xi_ref[...].astype(BF16), cim_ref[...], preferred_element_type=F32)
         + u * d_ref[...])
    y = _gelu(y)
    gate = jnp.dot(y.astype(BF16), wglu_ref[...], preferred_element_type=F32) + bglu_ref[...]
    return y * jax.nn.sigmoid(gate)


def _s5_project(u, bre_ref, bim_ref, xr_ref, xi_ref):
    xr_ref[...] = jnp.dot(u, bre_ref[...], preferred_element_type=F32, precision=lax.Precision.HIGHEST)
    xi_ref[...] = jnp.dot(u, bim_ref[...], preferred_element_type=F32, precision=lax.Precision.HIGHEST)


def _s5_seq_kernel(u_ref, bre_ref, bim_ref, cre_ref, cim_ref, d_ref, wglu_ref, bglu_ref, p_ref, x0_ref,
                   o_ref, st_ref, xr_ref, xi_ref, car_ref):
    ts = u_ref.shape[1]
    u = u_ref[0]
    _s5_project(u, bre_ref, bim_ref, xr_ref, xi_ref)

    @pl.when(pl.program_id(1) == 0)
    def _():
        car_ref[...] = x0_ref[0]

    pr = p_ref[0]
    pi = p_ref[1]
    row = lax.broadcasted_iota(jnp.int32, (S5_SCAN_ROWS, S5_NCH), 0)

    def blk(j, carry):
        cr, ci = carry
        r0 = pl.multiple_of(j * S5_SCAN_ROWS, S5_SCAN_ROWS)
        xr, xi = _s5_scan_block(xr_ref[pl.ds(r0, S5_SCAN_ROWS), :], xi_ref[pl.ds(r0, S5_SCAN_ROWS), :],
                                pr, pi, cr, ci, row)
        xr_ref[pl.ds(r0, S5_SCAN_ROWS), :] = xr
        xi_ref[pl.ds(r0, S5_SCAN_ROWS), :] = xi
        return xr[S5_SCAN_ROWS - 1:], xi[S5_SCAN_ROWS - 1:]

    cr, ci = lax.fori_loop(0, ts // S5_SCAN_ROWS, blk, (car_ref[0:1], car_ref[1:2]))
    car_ref[0:1] = cr
    car_ref[1:2] = ci
    st_ref[0] = car_ref[...]
    o_ref[0] = _s5_readout(u, xr_ref, xi_ref, cre_ref, cim_ref, d_ref, wglu_ref, bglu_ref)


def _s5_short_kernel(u_ref, bre_ref, bim_ref, cre_ref, cim_ref, d_ref, wglu_ref, bglu_ref, p_ref, x0_ref,
                     o_ref, st_ref, xr_ref, xi_ref):
    nseq = x0_ref.shape[0]
    u = u_ref[...]
    _s5_project(u, bre_ref, bim_ref, xr_ref, xi_ref)
    pr = p_ref[0]
    pi = p_ref[1]
    row = lax.broadcasted_iota(jnp.int32, (S5_SCAN_ROWS, S5_NCH), 0)

    def blk(j, carry):
        r0 = pl.multiple_of(j * S5_SCAN_ROWS, S5_SCAN_ROWS)
        x0 = x0_ref[j]
        xr, xi = _s5_scan_block(xr_ref[pl.ds(r0, S5_SCAN_ROWS), :], xi_ref[pl.ds(r0, S5_SCAN_ROWS), :],
                                pr, pi, x0[0:1], x0[1:2], row)
        xr_ref[pl.ds(r0, S5_SCAN_ROWS), :] = xr
        xi_ref[pl.ds(r0, S5_SCAN_ROWS), :] = xi
        st_ref[j, 0:1, :] = xr[S5_SCAN_ROWS - 1:]
        st_ref[j, 1:2, :] = xi[S5_SCAN_ROWS - 1:]
        return carry

    lax.fori_loop(0, nseq, blk, 0)
    o_ref[...] = _s5_readout(u, xr_ref, xi_ref, cre_ref, cim_ref, d_ref, wglu_ref, bglu_ref)


def _s5_tables(a_re, a_im, log_step, b_re, b_im, c_re, c_im):
    step = jnp.exp(log_step.astype(F32))[:, None]
    mag = jnp.exp(a_re * step)
    abr = mag * jnp.cos(a_im * step)
    abi = mag * jnp.sin(a_im * step)
    den = a_re * a_re + a_im * a_im
    cr = ((abr - 1.0) * a_re + abi * a_im) / den
    ci = (abi * a_re - (abr - 1.0) * a_im) / den
    bbr = cr[..., None] * b_re - ci[..., None] * b_im
    bbi = cr[..., None] * b_im + ci[..., None] * b_re
    eye = jnp.eye(S5_GROUPS, dtype=F32)
    bre = jnp.einsum('gpi,gh->gihp', bbr, eye).reshape(S5_CH, S5_NCH)
    bim = jnp.einsum('gpi,gh->gihp', bbi, eye).reshape(S5_CH, S5_NCH)
    cre = jnp.einsum('gop,gh->gpho', c_re, eye).reshape(S5_NCH, S5_CH).astype(BF16)
    cim = jnp.einsum('gop,gh->gpho', c_im, eye).reshape(S5_NCH, S5_CH).astype(BF16)
    ar = abr.reshape(1, S5_NCH)
    ai = abi.reshape(1, S5_NCH)
    pws = [(ar, ai)]
    for _ in range(S5_SCAN_ROWS - 1):
        qr, qi = pws[-1]
        pws.append((qr * ar - qi * ai, qr * ai + qi * ar))
    powers = jnp.stack([jnp.concatenate([p[0] for p in pws], axis=0),
                        jnp.concatenate([p[1] for p in pws], axis=0)], axis=0)
    return bre, bim, cre, cim, powers


def s5_layer(u, a_re, a_im, log_step, b_re, b_im, c_re, c_im, d, w_glu, b_glu, x0_re, x0_im):
    bsz, seq, _ = u.shape
    bre, bim, cre, cim, powers = _s5_tables(a_re, a_im, log_step, b_re, b_im, c_re, c_im)
    x0 = jnp.stack([x0_re.reshape(bsz, S5_NCH), x0_im.reshape(bsz, S5_NCH)], axis=1)
    d2 = d.astype(F32).reshape(1, S5_CH)
    bg = b_glu.astype(F32).reshape(1, S5_CH)
    wg = w_glu.astype(BF16)
    const = lambda *_: (0, 0)
    weight_specs = [pl.BlockSpec((S5_CH, S5_NCH), const), pl.BlockSpec((S5_CH, S5_NCH), const),
                    pl.BlockSpec((S5_NCH, S5_CH), const), pl.BlockSpec((S5_NCH, S5_CH), const),
                    pl.BlockSpec((1, S5_CH), const), pl.BlockSpec((S5_CH, S5_CH), const),
                    pl.BlockSpec((1, S5_CH), const),
                    pl.BlockSpec((2, S5_SCAN_ROWS, S5_NCH), lambda *_: (0, 0, 0))]
    weights = (bre, bim, cre, cim, d2, wg, bg, powers)
    if seq == S5_SCAN_ROWS:
        nseq = math.gcd(bsz, 64)
        rows = nseq * seq
        out, st = pl.pallas_call(
            _s5_short_kernel,
            grid=(bsz // nseq,),
            in_specs=[pl.BlockSpec((rows, S5_CH), lambda i: (i, 0))] + weight_specs
                     + [pl.BlockSpec((nseq, 2, S5_NCH), lambda i: (i, 0, 0))],
            out_specs=[pl.BlockSpec((rows, S5_CH), lambda i: (i, 0)),
                       pl.BlockSpec((nseq, 2, S5_NCH), lambda i: (i, 0, 0))],
            out_shape=[jax.ShapeDtypeStruct((bsz * seq, S5_CH), F32),
                       jax.ShapeDtypeStruct((bsz, 2, S5_NCH), F32)],
            scratch_shapes=[pltpu.VMEM((rows, S5_NCH), F32), pltpu.VMEM((rows, S5_NCH), F32)],
            compiler_params=pltpu.CompilerParams(dimension_semantics=("arbitrary",),
                                                 vmem_limit_bytes=V7X_VMEM_LIMIT),
            name="s5_short",
        )(u.reshape(bsz * seq, S5_CH), *weights, x0)
        out = out.reshape(bsz, seq, S5_CH)
    else:
        ts = math.gcd(seq, 512)
        out, st = pl.pallas_call(
            _s5_seq_kernel,
            grid=(bsz, seq // ts),
            in_specs=[pl.BlockSpec((1, ts, S5_CH), lambda b, s: (b, s, 0))] + weight_specs
                     + [pl.BlockSpec((1, 2, S5_NCH), lambda b, s: (b, 0, 0))],
            out_specs=[pl.BlockSpec((1, ts, S5_CH), lambda b, s: (b, s, 0)),
                       pl.BlockSpec((1, 2, S5_NCH), lambda b, s: (b, 0, 0))],
            out_shape=[jax.ShapeDtypeStruct((bsz, seq, S5_CH), F32),
                       jax.ShapeDtypeStruct((bsz, 2, S5_NCH), F32)],
            scratch_shapes=[pltpu.VMEM((ts, S5_NCH), F32), pltpu.VMEM((ts, S5_NCH), F32),
                            pltpu.VMEM((2, S5_NCH), F32)],
            compiler_params=pltpu.CompilerParams(dimension_semantics=("arbitrary", "arbitrary"),
                                                 vmem_limit_bytes=V7X_VMEM_LIMIT),
            name="s5_seq",
        )(u, *weights, x0)
    s_re = st[:, 0].reshape(bsz, S5_GROUPS, S5_STATE)
    s_im = st[:, 1].reshape(bsz, S5_GROUPS, S5_STATE)
    return out, s_re, s_im


def mla_keys(lat, krope, w_uk, w_uv, g_kn):
    k_nope = rmsnorm(jnp.einsum('bsr,rhd->bshd', lat, w_uk), g_kn)
    v = jnp.einsum('bsr,rhd->bshd', lat, w_uv)
    kr = jnp.broadcast_to(krope[:, :, None, :], k_nope.shape[:3] + (MLA_ROPE,)).astype(k_nope.dtype)
    return jnp.concatenate([k_nope, kr], axis=-1), v


PEER_HALF = PEER_KEY_DIM // 2
PEER_QW = PEER_HEADS * PEER_KEY_DIM
PEER_ROUTE_TOKENS = 256
PEER_TOKENS = 512
PEER_KEY_ROWS = 8
PEER_EXPERT_BLOCK = PEER_KEY_ROWS * PEER_NKEYS
PEER_ROW_BLOCK = 32


def _extract_top(xs, rows_f):
    slot = lax.broadcasted_iota(jnp.int32, (PEER_TOPK, V7X_LANES), 0)
    nrows = float(xs[0].shape[0])

    def body(r, carry):
        out = []
        for x, sv in carry:
            m = jnp.max(x, axis=0, keepdims=True)
            first = jnp.min(jnp.where(x == m, rows_f, nrows), axis=0, keepdims=True)
            out.append((jnp.where(rows_f == first, -jnp.inf, x), jnp.where(slot == r, m, sv)))
        return tuple(out)

    init = tuple((x, jnp.zeros((PEER_TOPK, V7X_LANES), F32)) for x in xs)
    return [sv for _, sv in lax.fori_loop(0, PEER_TOPK, body, init)]


PEER_PAIR_PIECES = tuple((b, PEER_TOPK // (b + 1)) for b in range(1, 8))
PEER_PAIR_ROWS = PEER_TOPK + 8 * len(PEER_PAIR_PIECES) + 8


def _pair_candidates(sv0, sv1):
    row8 = lax.broadcasted_iota(jnp.int32, (8, V7X_LANES), 0)
    pieces = [sv0 + sv1[0:1]]
    for b, n_valid in PEER_PAIR_PIECES:
        pieces.append(jnp.where(row8 < n_valid, sv0[0:8] + sv1[b:b + 1], -jnp.inf))
    pieces.append(sv0[0:1] + sv1[8:16])
    return jnp.concatenate(pieces, axis=0)


def _peer_route_kernel(h_ref, g_ref, wqt_ref, sk_ref, xb_ref, s0_ref, scl_ref, s1_ref, e1_ref, tau_ref,
                       st_ref):
    tr = h_ref.shape[0]
    hf = h_ref[...]
    c = hf * lax.rsqrt(jnp.mean(hf * hf, axis=-1, keepdims=True) + EPS) * g_ref[...]
    cb = c.astype(BF16)
    xb_ref[...] = cb
    qt = lax.dot_general(wqt_ref[...], cb, (((1,), (1,)), ((), ())), preferred_element_type=F32)
    for hc in range(2 * PEER_HEADS):
        st_ref[hc] = jnp.dot(sk_ref[hc], qt[hc * PEER_HALF:(hc + 1) * PEER_HALF].astype(BF16),
                             preferred_element_type=F32)

    rows_key = lax.broadcasted_iota(jnp.int32, (PEER_NKEYS, V7X_LANES), 0).astype(F32)
    rows_pair = lax.broadcasted_iota(jnp.int32, (PEER_PAIR_ROWS, V7X_LANES), 0).astype(F32)
    lane_groups = [slice(lg * V7X_LANES, (lg + 1) * V7X_LANES) for lg in range(tr // V7X_LANES)]

    def head(h, carry):
        tops = []
        for lanes in lane_groups:
            s0 = st_ref[2 * h, :, lanes]
            s1 = st_ref[2 * h + 1, :, lanes]
            tops.append(_extract_top([s0, s1], rows_key))
        bests = _extract_top([_pair_candidates(sv0, sv1) for sv0, sv1 in tops], rows_pair)
        for lanes, (sv0, sv1), best in zip(lane_groups, tops, bests):
            s0 = st_ref[2 * h, :, lanes]
            s1 = st_ref[2 * h + 1, :, lanes]
            z = jnp.sum(jnp.exp(best - best[0:1]), axis=0, keepdims=True)
            s0_ref[h, :, lanes] = s0
            scl_ref[h, :, lanes] = jnp.exp(s0 - sv0[0:1]) / z
            s1_ref[h, :, lanes] = s1
            e1_ref[h, :, lanes] = jnp.exp(s1 - sv1[0:1])
            tau_ref[h, :, lanes] = best[PEER_TOPK - 1:PEER_TOPK]
        return carry

    lax.fori_loop(0, PEER_HEADS, head, 0)


def _peer_expert_kernel(x_ref, u_ref, vt_ref, s0_ref, scl_ref, s1_ref, e1_ref, tau_ref, o_ref,
                        acc_ref, a_ref, s_ref):
    ei = pl.program_id(1)
    tt = x_ref.shape[0]
    key_rows = u_ref.shape[0] // PEER_NKEYS

    @pl.when(ei == 0)
    def _():
        acc_ref[...] = jnp.zeros_like(acc_ref)

    s_ref[...] = lax.dot_general(u_ref[...], x_ref[...], (((1,), (1,)), ((), ())), preferred_element_type=F32)

    assert key_rows == PEER_KEY_ROWS
    first_keys = pl.ds(pl.multiple_of(ei * PEER_KEY_ROWS, PEER_KEY_ROWS), PEER_KEY_ROWS)
    for lg in range(tt // V7X_LANES):
        lanes = slice(lg * V7X_LANES, (lg + 1) * V7X_LANES)
        s0rows = [s0_ref[h, first_keys, lanes] for h in range(PEER_HEADS)]
        sclrows = [scl_ref[h, first_keys, lanes] for h in range(PEER_HEADS)]
        taurow = [tau_ref[h, :, lanes] for h in range(PEER_HEADS)]
        for r in range(key_rows):
            s0row = [v[r:r + 1] for v in s0rows]
            sclrow = [v[r:r + 1] for v in sclrows]
            for jb in range(PEER_NKEYS // PEER_ROW_BLOCK):
                rs = slice(jb * PEER_ROW_BLOCK, (jb + 1) * PEER_ROW_BLOCK)
                es = slice(r * PEER_NKEYS + jb * PEER_ROW_BLOCK, r * PEER_NKEYS + (jb + 1) * PEER_ROW_BLOCK)
                gate = jnp.zeros((PEER_ROW_BLOCK, V7X_LANES), F32)
                for h in range(PEER_HEADS):
                    pair = s1_ref[h, rs, lanes] + s0row[h]
                    gate = gate + jnp.where(pair >= taurow[h], e1_ref[h, rs, lanes], 0.0) * sclrow[h]
                a_ref[es, lanes] = (gate * _gelu(s_ref[es, lanes])).astype(BF16)

    acc_ref[...] += jnp.dot(vt_ref[...], a_ref[...], preferred_element_type=F32)

    @pl.when(ei == pl.num_programs(1) - 1)
    def _():
        o_ref[...] = acc_ref[...].T


def _peer_tables_kernel(u_ref, v_ref, ub_ref, vt_ref):
    ub_ref[...] = u_ref[...].astype(BF16)
    vt_ref[...] = v_ref[...].T.astype(BF16)


def peer_tables(u_tab, v_tab):
    nexp, dm = u_tab.shape
    te = math.gcd(nexp, 512)
    return pl.pallas_call(
        _peer_tables_kernel,
        grid=(nexp // te,),
        in_specs=[pl.BlockSpec((te, dm), lambda i: (i, 0)), pl.BlockSpec((te, dm), lambda i: (i, 0))],
        out_specs=[pl.BlockSpec((te, dm), lambda i: (i, 0)), pl.BlockSpec((dm, te), lambda i: (0, i))],
        out_shape=[jax.ShapeDtypeStruct((nexp, dm), BF16), jax.ShapeDtypeStruct((dm, nexp), BF16)],
        compiler_params=pltpu.CompilerParams(dimension_semantics=("arbitrary",),
                                             vmem_limit_bytes=V7X_VMEM_LIMIT),
        name="peer_tables",
    )(u_tab, v_tab)


def peer_ffn(hres, g_ffn, w_q, subkeys, ub, vt):
    bsz, seq, dm = hres.shape
    ntok = bsz * seq
    nexp = ub.shape[0]
    tr = math.gcd(ntok, PEER_ROUTE_TOKENS)
    tt = math.gcd(ntok, PEER_TOKENS)
    wqt = w_q.T.astype(BF16)
    sk = subkeys.reshape(2 * PEER_HEADS, PEER_NKEYS, PEER_HALF).astype(BF16)
    key_shape = (PEER_HEADS, PEER_NKEYS, ntok)
    key_block = lambda t: pl.BlockSpec((PEER_HEADS, PEER_NKEYS, t), lambda i, *_: (0, 0, i))
    tau_block = lambda t: pl.BlockSpec((PEER_HEADS, 1, t), lambda i, *_: (0, 0, i))
    xb, s0t, sclt, s1t, e1t, taut = pl.pallas_call(
        _peer_route_kernel,
        grid=(ntok // tr,),
        in_specs=[pl.BlockSpec((tr, dm), lambda i: (i, 0)),
                  pl.BlockSpec((1, dm), lambda i: (0, 0)),
                  pl.BlockSpec((PEER_QW, dm), lambda i: (0, 0)),
                  pl.BlockSpec((2 * PEER_HEADS, PEER_NKEYS, PEER_HALF), lambda i: (0, 0, 0))],
        out_specs=[pl.BlockSpec((tr, dm), lambda i: (i, 0)),
                   key_block(tr), key_block(tr), key_block(tr), key_block(tr), tau_block(tr)],
        out_shape=[jax.ShapeDtypeStruct((ntok, dm), BF16)] + [jax.ShapeDtypeStruct(key_shape, F32)] * 4
                  + [jax.ShapeDtypeStruct((PEER_HEADS, 1, ntok), F32)],
        scratch_shapes=[pltpu.VMEM((2 * PEER_HEADS, PEER_NKEYS, tr), F32)],
        compiler_params=pltpu.CompilerParams(dimension_semantics=("arbitrary",),
                                             vmem_limit_bytes=V7X_VMEM_LIMIT),
        name="peer_route",
    )(hres.reshape(ntok, dm), g_ffn.astype(F32).reshape(1, dm), wqt, sk)

    eb = PEER_EXPERT_BLOCK
    out = pl.pallas_call(
        _peer_expert_kernel,
        grid=(ntok // tt, nexp // eb),
        in_specs=[pl.BlockSpec((tt, dm), lambda t, e: (t, 0)),
                  pl.BlockSpec((eb, dm), lambda t, e: (e, 0)),
                  pl.BlockSpec((dm, eb), lambda t, e: (0, e)),
                  key_block(tt), key_block(tt), key_block(tt), key_block(tt), tau_block(tt)],
        out_specs=pl.BlockSpec((tt, dm), lambda t, e: (t, 0)),
        out_shape=jax.ShapeDtypeStruct((ntok, dm), F32),
        scratch_shapes=[pltpu.VMEM((dm, tt), F32), pltpu.VMEM((eb, tt), BF16), pltpu.VMEM((eb, tt), F32)],
        compiler_params=pltpu.CompilerParams(dimension_semantics=("arbitrary", "arbitrary"),
                                             vmem_limit_bytes=V7X_VMEM_LIMIT),
        name="peer_experts",
    )(xb, ub, vt, s0t, sclt, s1t, e1t, taut)
    return out.reshape(bsz, seq, dm)


def mixing_sublayer(a, pos, lp, past):
    bsz, seq, _ = a.shape
    z = matmul3(a, lp['w_in'])
    points = [int(t) for t in np.cumsum(SPLIT_SIZES)[:-1]]
    (fq, fk, fv, ff, mq, mk, mv, mi, mf, mo, su, cq, ckv, kr) = jnp.split(z, points, axis=-1)

    q_f = rmsnorm(fq.reshape(bsz, seq, FOX_KV_HEADS, FOX_GROUP, HEAD_DIM), lp['g_fox_q'])
    k_f = rmsnorm(fk.reshape(bsz, seq, FOX_KV_HEADS, HEAD_DIM), lp['g_fox_k'])
    v_f = fv.reshape(bsz, seq, FOX_KV_HEADS, HEAD_DIM)
    lf_f = jax.nn.log_sigmoid(ff.astype(F32) + lp['b_fox_f'].astype(F32))
    lf_g = lf_f.reshape(bsz, seq, FOX_KV_HEADS, FOX_GROUP)

    c_q = rmsnorm(cq, lp['g_mla_cq'])
    q_m = jnp.einsum('bsr,rhd->bshd', c_q, lp['w_mla_uq'])
    q_m = jnp.concatenate([rmsnorm(q_m[..., :MLA_NOPE], lp['g_mla_qn']),
                           rope(rmsnorm(q_m[..., MLA_NOPE:], lp['g_mla_qr']), pos)], axis=-1)[:, :, :, None, :]
    lat = rmsnorm(ckv, lp['g_mla_ckv'])
    krope = rope(rmsnorm(kr, lp['g_mla_kr'])[:, :, None, :], pos)[:, :, 0, :]
    k_m, v_m = mla_keys(lat, krope, lp['w_mla_uk'], lp['w_mla_uv'], lp['g_mla_kn'])

    if past is None:
        o_f = prompt_attention(q_f, k_f, v_f, jnp.cumsum(lf_g, axis=1))
        o_m = prompt_attention(q_m, k_m, v_m, None)
        c0 = jnp.zeros((bsz, MLSTM_HEADS, HEAD_DIM, HEAD_DIM), F32)
        n0 = jnp.zeros((bsz, MLSTM_HEADS, HEAD_DIM), F32)
        m0 = jnp.zeros((bsz, MLSTM_HEADS), F32)
        x0_re = jnp.zeros((bsz, S5_GROUPS, S5_STATE), F32)
        x0_im = jnp.zeros((bsz, S5_GROUPS, S5_STATE), F32)
    else:
        o_f = decode_attention(q_f, k_f, v_f, lf_g, past['load_fox'], past['n_blocks'])
        o_m = decode_attention(q_m, k_m, v_m, None, past['load_mla'], past['n_blocks'])
        c0, n0, m0, x0_re, x0_im = past['rec']

    ig = mi.astype(F32) + lp['b_mlstm_i'].astype(F32)
    lf_l = jax.nn.log_sigmoid(mf.astype(F32) + lp['b_mlstm_f'].astype(F32))
    h_l, c1, n1, m1 = mlstm_chunkwise(mq.reshape(bsz, seq, MLSTM_HEADS, HEAD_DIM),
                                      mk.reshape(bsz, seq, MLSTM_HEADS, HEAD_DIM),
                                      mv.reshape(bsz, seq, MLSTM_HEADS, HEAD_DIM), ig, lf_l, c0, n0, m0)
    o_l = rmsnorm(h_l, lp['g_mlstm_h']).astype(a.dtype).reshape(bsz, seq, MLSTM_W) * jax.nn.sigmoid(mo)

    o_s, s_re, s_im = s5_layer(su, lp['s5_a_re'], lp['s5_a_im'], lp['s5_log_step'], lp['s5_b_re'], lp['s5_b_im'],
                               lp['s5_c_re'], lp['s5_c_im'], lp['s5_d'], lp['w_glu'], lp['b_glu'], x0_re, x0_im)

    mix = jnp.concatenate([o_f.reshape(bsz, seq, -1), o_l, o_s, o_m.reshape(bsz, seq, -1)], axis=-1)
    out = matmul3(mix, lp['w_out'])
    return out, (k_f, v_f, lf_f, lat, krope, c1, n1, m1, s_re, s_im)


def run_trunk(x, p, pos, layer_params, pasts):
    h = x
    states = []
    for i in range(DEPTH):
        lp = layer_params[i]
        mix, st = mixing_sublayer(rmsnorm(h, lp['g_mix']), pos, lp, pasts[i])
        h = h + mix
        h = h + peer_ffn(h, lp['g_ffn'], lp['w_peer_q'], lp['peer_subkeys'], lp['peer_ub'], lp['peer_vt'])
        gate = jax.nn.sigmoid(matmul3(rmsnorm(h, lp['g_ple']), lp['w_ple_gate']))
        h = h + matmul3(p[i], lp['w_ple']) * gate
        states.append(st)
    stacked = [jnp.stack([st[j] for st in states], axis=0) for j in range(len(states[0]))]
    return h, stacked


def kernel(x_prompt, x_sample, cache_fox_k, cache_fox_v, cache_fox_logf, cache_mla_latent, cache_mla_krope,
           state_mlstm_C, state_mlstm_n, state_mlstm_m, state_s5_re, state_s5_im, page_table, p_prompt, p_sample,
           g_mix, w_in, g_fox_q, g_fox_k, b_fox_f, b_mlstm_i, b_mlstm_f, g_mlstm_h,
           s5_a_re, s5_a_im, s5_log_step, s5_b_re, s5_b_im, s5_c_re, s5_c_im, s5_d, w_glu, b_glu,
           g_mla_cq, w_mla_uq, g_mla_qn, g_mla_qr, g_mla_ckv, g_mla_kr, w_mla_uk, w_mla_uv, g_mla_kn,
           w_out, g_ffn, w_peer_q, peer_subkeys, peer_u, peer_v, g_ple, w_ple_gate, w_ple):
    names = ('g_mix', 'w_in', 'g_fox_q', 'g_fox_k', 'b_fox_f', 'b_mlstm_i', 'b_mlstm_f', 'g_mlstm_h',
             's5_a_re', 's5_a_im', 's5_log_step', 's5_b_re', 's5_b_im', 's5_c_re', 's5_c_im', 's5_d', 'w_glu',
             'b_glu', 'g_mla_cq', 'w_mla_uq', 'g_mla_qn', 'g_mla_qr', 'g_mla_ckv', 'g_mla_kr', 'w_mla_uk',
             'w_mla_uv', 'g_mla_kn', 'w_out', 'g_ffn', 'w_peer_q', 'peer_subkeys', 'peer_u', 'peer_v', 'g_ple',
             'w_ple_gate', 'w_ple')
    vals = (g_mix, w_in, g_fox_q, g_fox_k, b_fox_f, b_mlstm_i, b_mlstm_f, g_mlstm_h,
            s5_a_re, s5_a_im, s5_log_step, s5_b_re, s5_b_im, s5_c_re, s5_c_im, s5_d, w_glu,
            b_glu, g_mla_cq, w_mla_uq, g_mla_qn, g_mla_qr, g_mla_ckv, g_mla_kr, w_mla_uk,
            w_mla_uv, g_mla_kn, w_out, g_ffn, w_peer_q, peer_subkeys, peer_u, peer_v, g_ple,
            w_ple_gate, w_ple)
    layer_params = [{n: v[i] for n, v in zip(names, vals)} for i in range(DEPTH)]
    for lp in layer_params:
        lp['peer_ub'], lp['peer_vt'] = peer_tables(lp['peer_u'], lp['peer_v'])

    pos_p = jnp.arange(x_prompt.shape[1], dtype=jnp.int32)
    y_prompt, st_p = run_trunk(x_prompt, p_prompt, pos_p, layer_params, [None] * DEPTH)

    n_pages = page_table.shape[1]
    past_len = n_pages * PAGE_SIZE
    bp = math.gcd(n_pages, MAX_KEY_BLOCK_PAGES)
    n_blocks = n_pages // bp
    rows = bp * PAGE_SIZE
    db = x_sample.shape[0]

    def make_past(i):
        def pages_of(j):
            return lax.dynamic_slice_in_dim(page_table, j * bp, bp, axis=1)

        def load_fox(j):
            pg = pages_of(j)
            k = cache_fox_k[i, pg].reshape(db, rows, FOX_KV_HEADS, HEAD_DIM)
            v = cache_fox_v[i, pg].reshape(db, rows, FOX_KV_HEADS, HEAD_DIM)
            lf = cache_fox_logf[i, pg].reshape(db, rows, FOX_KV_HEADS, FOX_GROUP).astype(F32)
            return k, v, lf

        def load_mla(j):
            pg = pages_of(j)
            lat = cache_mla_latent[i, pg].reshape(db, rows, MLA_KV_RANK)
            kr = cache_mla_krope[i, pg].reshape(db, rows, MLA_ROPE)
            k, v = mla_keys(lat, kr, w_mla_uk[i], w_mla_uv[i], g_mla_kn[i])
            return k, v, None

        return {'load_fox': load_fox, 'load_mla': load_mla, 'n_blocks': n_blocks,
                'rec': (state_mlstm_C[i], state_mlstm_n[i], state_mlstm_m[i], state_s5_re[i], state_s5_im[i])}

    pasts = [make_past(i) for i in range(DEPTH)]
    pos_s = past_len + jnp.arange(x_sample.shape[1], dtype=jnp.int32)
    y_sample, st_s = run_trunk(x_sample, p_sample, pos_s, layer_params, pasts)

    (pk, pv, plf, plat, pkr, pc, pn, pm, pre, pim) = st_p
    (sk, sv, slf, slat, skr, sc, sn, sm, sre, sim) = st_s
    return (y_prompt, y_sample, pk, pv, plf, plat, pkr, pc, pn, pm, pre, pim,
            sk, sv, slf, slat, skr, sc, sn, sm, sre, sim)
```

```python
import functools
import math

import jax
import jax.numpy as jnp
import numpy as np
from jax import lax
from jax.experimental import pallas as pl
from jax.experimental.pallas import tpu as pltpu

D_MODEL = 1024
DEPTH = 2
PAGE_SIZE = 128
HEAD_DIM = 64
GROUP_WIDTH = D_MODEL // 4
FOX_HEADS = GROUP_WIDTH // HEAD_DIM
FOX_KV_HEADS = FOX_HEADS // 2
FOX_GROUP = FOX_HEADS // FOX_KV_HEADS
MLSTM_HEADS = GROUP_WIDTH // HEAD_DIM
MLSTM_W = MLSTM_HEADS * HEAD_DIM
MLSTM_CHUNK = 64
S5_CH = GROUP_WIDTH
S5_GROUP_SIZE = 16
S5_GROUPS = S5_CH // S5_GROUP_SIZE
S5_STATE = 64
MLA_HEADS = GROUP_WIDTH // HEAD_DIM
MLA_NOPE = HEAD_DIM
MLA_ROPE = HEAD_DIM // 2
MLA_V = HEAD_DIM
MLA_Q_RANK = D_MODEL // 8
MLA_KV_RANK = D_MODEL // 8
ROPE_THETA = 10000.0
PEER_HEADS = 8
PEER_NKEYS = 128
PEER_KEY_DIM = 256
PEER_TOPK = 16
PEER_TOKEN_BLOCK = 256
Q_BLOCK = 128
MAX_KEY_BLOCK_PAGES = 8
EPS = 1e-6
F32 = jnp.float32
BF16 = jnp.bfloat16

SPLIT_SIZES = (FOX_HEADS * HEAD_DIM, FOX_KV_HEADS * HEAD_DIM, FOX_KV_HEADS * HEAD_DIM, FOX_HEADS,
               MLSTM_W, MLSTM_W, MLSTM_W, MLSTM_HEADS, MLSTM_HEADS, MLSTM_W,
               S5_CH,
               MLA_Q_RANK, MLA_KV_RANK, MLA_ROPE)

V7X_LANES = 128
V7X_VMEM_LIMIT = 48 * 1024 * 1024


def _mm_kernel(x_ref, w_ref, o_ref):
    o_ref[...] = jnp.dot(x_ref[...].astype(BF16), w_ref[...], preferred_element_type=F32)


def _row_tile(n):
    for t in (512, 256, 128, 64, 32, 16, 8):
        if n % t == 0:
            return t
    raise ValueError(f"row count {n} is not a multiple of 8")


def matmul(x, w):
    n, k = x.shape
    m = w.shape[1]
    m_pad = -(-m // V7X_LANES) * V7X_LANES
    wb = w.astype(BF16)
    if m_pad != m:
        wb = jnp.pad(wb, ((0, 0), (0, m_pad - m)))
    tm = _row_tile(n)
    out = pl.pallas_call(
        _mm_kernel,
        grid=(n // tm,),
        in_specs=[pl.BlockSpec((tm, k), lambda i: (i, 0)),
                  pl.BlockSpec((k, m_pad), lambda i: (0, 0))],
        out_specs=pl.BlockSpec((tm, m_pad), lambda i: (i, 0)),
        out_shape=jax.ShapeDtypeStruct((n, m_pad), F32),
        compiler_params=pltpu.CompilerParams(dimension_semantics=("arbitrary",),
                                             vmem_limit_bytes=V7X_VMEM_LIMIT),
        name="matmul",
    )(x, wb)
    return out[:, :m] if m_pad != m else out


def matmul3(x, w):
    b, s, k = x.shape
    return matmul(x.reshape(b * s, k), w).reshape(b, s, w.shape[1])


def rmsnorm(x, g):
    xf = x.astype(F32)
    y = xf * lax.rsqrt(jnp.mean(xf * xf, axis=-1, keepdims=True) + EPS)
    return (y * g.astype(F32)).astype(x.dtype)


def rope(x, pos):
    half = x.shape[-1] // 2
    inv_freq = ROPE_THETA ** (-jnp.arange(half, dtype=F32) / half)
    ang = pos.astype(F32)[:, None] * inv_freq[None, :]
    cos = jnp.cos(ang)[None, :, None, :]
    sin = jnp.sin(ang)[None, :, None, :]
    xf = x.astype(F32)
    x1, x2 = xf[..., :half], xf[..., half:]
    return jnp.concatenate([x1 * cos - x2 * sin, x2 * cos + x1 * sin], axis=-1).astype(x.dtype)


ATTN_BLOCK = 512


def _flash_kernel(q_ref, k_ref, v_ref, fq_ref, fk_ref, o_ref, m_ref, l_ref, acc_ref, *, scale, use_forget):
    qi = pl.program_id(2)
    ki = pl.program_id(3)

    @pl.when(ki == 0)
    def _():
        m_ref[...] = jnp.full_like(m_ref, -jnp.inf)
        l_ref[...] = jnp.zeros_like(l_ref)
        acc_ref[...] = jnp.zeros_like(acc_ref)

    def step(diagonal):
        s = lax.dot_general(q_ref[0, 0], k_ref[0, 0], (((1,), (1,)), ((), ())),
                            preferred_element_type=F32) * scale
        if use_forget:
            s = s + fq_ref[0, 0] - fk_ref[0, 0]
        if diagonal:
            row = lax.broadcasted_iota(jnp.int32, s.shape, 0)
            col = lax.broadcasted_iota(jnp.int32, s.shape, 1)
            s = jnp.where(row >= col, s, -jnp.inf)
        m_prev = m_ref[...]
        m_new = jnp.maximum(m_prev, jnp.max(s, axis=-1, keepdims=True))
        p = jnp.exp(s - m_new)
        corr = jnp.exp(m_prev - m_new)
        l_ref[...] = corr * l_ref[...] + jnp.sum(p, axis=-1, keepdims=True)
        acc_ref[...] = corr * acc_ref[...] + jnp.dot(p.astype(BF16), v_ref[0, 0], preferred_element_type=F32)
        m_ref[...] = m_new

    @pl.when(ki < qi)
    def _():
        step(False)

    @pl.when(ki == qi)
    def _():
        step(True)
        o_ref[0, 0] = acc_ref[...] / l_ref[...]


def prompt_attention(q, k, v, fcum):
    bsz, seq, n_kv, n_grp, dk = q.shape
    dv = v.shape[-1]
    nh = n_kv * n_grp
    tb = math.gcd(seq, ATTN_BLOCK)
    nb = seq // tb
    qh = jnp.transpose(q.reshape(bsz, seq, nh, dk), (0, 2, 1, 3)).astype(BF16)
    kh = jnp.transpose(k, (0, 2, 1, 3)).astype(BF16)
    vh = jnp.transpose(v, (0, 2, 1, 3)).astype(BF16)
    use_forget = fcum is not None
    if use_forget:
        ft = jnp.transpose(fcum.reshape(bsz, seq, nh), (0, 2, 1)).astype(F32)
    else:
        ft = jnp.zeros((bsz, nh, seq), F32)
    fq = ft[..., None]
    fk = ft[:, :, None, :]
    out = pl.pallas_call(
        functools.partial(_flash_kernel, scale=dk ** -0.5, use_forget=use_forget),
        grid=(bsz, nh, nb, nb),
        in_specs=[pl.BlockSpec((1, 1, tb, dk), lambda b, h, i, j: (b, h, i, 0)),
                  pl.BlockSpec((1, 1, tb, dk), lambda b, h, i, j: (b, h // n_grp, jnp.minimum(i, j), 0)),
                  pl.BlockSpec((1, 1, tb, dv), lambda b, h, i, j: (b, h // n_grp, jnp.minimum(i, j), 0)),
                  pl.BlockSpec((1, 1, tb, 1), lambda b, h, i, j: (b, h, i, 0)),
                  pl.BlockSpec((1, 1, 1, tb), lambda b, h, i, j: (b, h, 0, jnp.minimum(i, j)))],
        out_specs=pl.BlockSpec((1, 1, tb, dv), lambda b, h, i, j: (b, h, i, 0)),
        out_shape=jax.ShapeDtypeStruct((bsz, nh, seq, dv), F32),
        scratch_shapes=[pltpu.VMEM((tb, 1), F32), pltpu.VMEM((tb, 1), F32), pltpu.VMEM((tb, dv), F32)],
        compiler_params=pltpu.CompilerParams(
            dimension_semantics=("arbitrary", "arbitrary", "arbitrary", "arbitrary"),
            vmem_limit_bytes=V7X_VMEM_LIMIT),
        name="flash_attention",
    )(qh, kh, vh, fq, fk)
    return jnp.transpose(out, (0, 2, 1, 3)).reshape(bsz, seq, n_kv, n_grp, dv)


DECODE_PAGES = 8
DECODE_ROWS = MLA_HEADS * 8


def _page_copies(pt_ref, layer, b, blk, slot, srcs, bufs, sems):
    out = []
    for p in range(DECODE_PAGES):
        page = pt_ref[b, blk * DECODE_PAGES + p]
        for src, buf, sem in zip(srcs, bufs, sems):
            out.append(pltpu.make_async_copy(src.at[layer, page], buf.at[slot, p], sem.at[slot]))
    return out


def _head_rows(x4):
    n = x4.shape[1]
    return jnp.concatenate([jnp.broadcast_to(x4[h:h + 1], (8, n)) for h in range(MLA_HEADS)], axis=0)


def _softmax_step(s, v_bf16, m_ref, l_ref, acc_ref):
    m_prev = m_ref[...]
    m_new = jnp.maximum(m_prev, jnp.max(s, axis=-1, keepdims=True))
    p = jnp.exp(s - m_new)
    corr = jnp.exp(m_prev - m_new)
    l_ref[...] = corr * l_ref[...] + jnp.sum(p, axis=-1, keepdims=True)
    acc_ref[...] = corr * acc_ref[...] + jnp.dot(p.astype(BF16), v_bf16, preferred_element_type=F32)
    m_ref[...] = m_new


def _decode_loop(pt_ref, layer, srcs, bufs, sems, block_fn, n_blocks, init):
    b = pl.program_id(0)
    for c in _page_copies(pt_ref, layer, b, n_blocks - 1, 0, srcs, bufs, sems):
        c.start()

    def body(it, carry):
        blk = n_blocks - 1 - it
        slot = jnp.bitwise_and(it, 1)

        @pl.when(it + 1 < n_blocks)
        def _():
            for c in _page_copies(pt_ref, layer, b, blk - 1, 1 - slot, srcs, bufs, sems):
                c.start()

        for c in _page_copies(pt_ref, layer, b, blk, slot, srcs, bufs, sems):
            c.wait()
        return block_fn(slot, carry)

    return lax.fori_loop(0, n_blocks, body, init)


def _fox_decode_kernel(pt_ref, q_ref, kn_ref, vn_ref, bn_ref, gq_ref, k_hbm, v_hbm, lf_hbm, o_ref,
                       kbuf, vbuf, lfbuf, sems, m_ref, l_ref, acc_ref, *, n_blocks, scale, layer):
    m_ref[...] = jnp.full_like(m_ref, -jnp.inf)
    l_ref[...] = jnp.zeros_like(l_ref)
    acc_ref[...] = jnp.zeros_like(acc_ref)
    q = q_ref[0]
    gq = gq_ref[0]
    lane = lax.broadcasted_iota(jnp.int32, (FOX_HEADS, V7X_LANES), 1)
    rows = DECODE_PAGES * PAGE_SIZE

    def block(slot, suf):
        biases = [None] * DECODE_PAGES
        for p in reversed(range(DECODE_PAGES)):
            x = lfbuf[slot, p]
            y = x
            for d in (1, 2, 4, 8, 16, 32, 64):
                y = y + jnp.where(lane < V7X_LANES - d, pltpu.roll(y, V7X_LANES - d, 1), 0.0)
            biases[p] = (y - x) + suf
            suf = suf + jnp.broadcast_to(y[:, 0:1], (FOX_HEADS, V7X_LANES))
        bias = jnp.concatenate([_head_rows(bp) for bp in biases], axis=1)
        kb = kbuf[slot].reshape(rows, V7X_LANES).astype(BF16)
        vb = vbuf[slot].reshape(rows, V7X_LANES).astype(BF16)
        s = lax.dot_general(q, kb, (((1,), (1,)), ((), ())), preferred_element_type=F32) * scale
        _softmax_step(s + bias + gq, vb, m_ref, l_ref, acc_ref)
        return suf

    _decode_loop(pt_ref, layer, (k_hbm, v_hbm, lf_hbm), (kbuf, vbuf, lfbuf), (sems.at[0], sems.at[1], sems.at[2]),
                 block, n_blocks, jnp.zeros((FOX_HEADS, V7X_LANES), F32))
    s = lax.dot_general(q, kn_ref[0], (((1,), (1,)), ((), ())), preferred_element_type=F32) * scale
    _softmax_step(s + bn_ref[0], vn_ref[0], m_ref, l_ref, acc_ref)
    o_ref[0] = acc_ref[...] / l_ref[...]


def fox_decode(q, k_new, v_new, lf_new, cache_k, cache_v, cache_lft, page_table, layer):
    bd, t, n_kv, n_grp, dk = q.shape
    nh = n_kv * n_grp
    assert nh * t == DECODE_ROWS and n_kv * dk == V7X_LANES and t == 8
    n_pages = page_table.shape[1]
    n_blocks = n_pages // DECODE_PAGES
    nl, npool = cache_k.shape[:2]
    qh = jnp.transpose(q.reshape(bd, t, nh, dk), (0, 2, 1, 3))
    zero = jnp.zeros((bd, n_grp * t, dk), q.dtype)
    qpad = jnp.concatenate(
        [jnp.concatenate([zero] * hk + [qh[:, hk * n_grp:(hk + 1) * n_grp].reshape(bd, n_grp * t, dk)]
                         + [zero] * (n_kv - 1 - hk), axis=-1) for hk in range(n_kv)], axis=1).astype(BF16)
    pad_rows = ((0, 0), (0, PAGE_SIZE - t), (0, 0))
    kn = jnp.pad(k_new.reshape(bd, t, n_kv * dk), pad_rows).astype(BF16)
    vn = jnp.pad(v_new.reshape(bd, t, n_kv * dk), pad_rows).astype(BF16)
    g = jnp.transpose(jnp.cumsum(lf_new.reshape(bd, t, nh).astype(F32), axis=1), (0, 2, 1))
    causal = jnp.tril(jnp.ones((t, t), bool))
    bn = jnp.where(causal, g[:, :, :, None] - g[:, :, None, :], -jnp.inf).reshape(bd, nh * t, t)
    bn = jnp.pad(bn, ((0, 0), (0, 0), (0, PAGE_SIZE - t)), constant_values=-jnp.inf)
    gq = g.reshape(bd, nh * t, 1)
    row_spec = lambda shape: pl.BlockSpec((1,) + shape, lambda b, pt: (b, 0, 0))
    out = pl.pallas_call(
        functools.partial(_fox_decode_kernel, n_blocks=n_blocks, scale=dk ** -0.5, layer=layer),
        grid_spec=pltpu.PrefetchScalarGridSpec(
            num_scalar_prefetch=1,
            grid=(bd,),
            in_specs=[row_spec((DECODE_ROWS, V7X_LANES)), row_spec((PAGE_SIZE, V7X_LANES)),
                      row_spec((PAGE_SIZE, V7X_LANES)), row_spec((DECODE_ROWS, PAGE_SIZE)),
                      row_spec((DECODE_ROWS, 1)),
                      pl.BlockSpec(memory_space=pl.ANY), pl.BlockSpec(memory_space=pl.ANY),
                      pl.BlockSpec(memory_space=pl.ANY)],
            out_specs=row_spec((DECODE_ROWS, V7X_LANES)),
            scratch_shapes=[pltpu.VMEM((2, DECODE_PAGES, PAGE_SIZE, V7X_LANES), F32),
                            pltpu.VMEM((2, DECODE_PAGES, PAGE_SIZE, V7X_LANES), F32),
                            pltpu.VMEM((2, DECODE_PAGES, nh, PAGE_SIZE), F32),
                            pltpu.SemaphoreType.DMA((3, 2)),
                            pltpu.VMEM((DECODE_ROWS, 1), F32), pltpu.VMEM((DECODE_ROWS, 1), F32),
                            pltpu.VMEM((DECODE_ROWS, V7X_LANES), F32)]),
        out_shape=jax.ShapeDtypeStruct((bd, DECODE_ROWS, V7X_LANES), F32),
        compiler_params=pltpu.CompilerParams(dimension_semantics=("arbitrary",),
                                             vmem_limit_bytes=V7X_VMEM_LIMIT),
        name="fox_decode",
    )(page_table, qpad, kn, vn, bn, gq, cache_k.reshape(nl, npool, PAGE_SIZE, n_kv * dk),
      cache_v.reshape(nl, npool, PAGE_SIZE, n_kv * dk), cache_lft)
    o = out.reshape(bd, n_kv, n_grp, t, n_kv, dk)
    o = jnp.stack([o[:, hk, :, :, hk] for hk in range(n_kv)], axis=1)
    return jnp.transpose(o, (0, 3, 1, 2, 4))


def _mla_decode_kernel(pt_ref, qn_ref, qr_ref, latn_ref, krn_ref, bn_ref, wuk_ref, seg_ref, wuv_ref,
                       lat_hbm, kr_hbm, o_ref, latbuf, krbuf, sems, m_ref, l_ref, acc_ref, *, n_blocks, scale, layer):
    m_ref[...] = jnp.full_like(m_ref, -jnp.inf)
    l_ref[...] = jnp.zeros_like(l_ref)
    acc_ref[...] = jnp.zeros_like(acc_ref)
    qn = qn_ref[0]
    qr = qr_ref[0]

    def scores(latb, krb):
        kn = jnp.dot(latb, wuk_ref[...], preferred_element_type=F32)
        ssq = lax.dot_general(seg_ref[...], (kn * kn).astype(BF16), (((1,), (1,)), ((), ())),
                              preferred_element_type=F32)
        inv = lax.rsqrt(ssq * (1.0 / MLA_NOPE) + EPS)
        s_nope = lax.dot_general(qn, kn.astype(BF16), (((1,), (1,)), ((), ())), preferred_element_type=F32)
        s_rope = lax.dot_general(qr, krb, (((1,), (1,)), ((), ())), preferred_element_type=F32)
        return (s_nope * _head_rows(inv) + s_rope) * scale

    rows = DECODE_PAGES * PAGE_SIZE

    def block(slot, carry):
        latb = latbuf[slot].reshape(rows, MLA_KV_RANK).astype(BF16)
        krb = krbuf[slot].reshape(rows, MLA_ROPE).astype(BF16)
        _softmax_step(scores(latb, krb), latb, m_ref, l_ref, acc_ref)
        return carry

    _decode_loop(pt_ref, layer, (lat_hbm, kr_hbm), (latbuf, krbuf), (sems.at[0], sems.at[1]), block, n_blocks,
                 jnp.zeros((8, V7X_LANES), F32))
    _softmax_step(scores(latn_ref[0], krn_ref[0]) + bn_ref[0], latn_ref[0], m_ref, l_ref, acc_ref)
    ctx = (acc_ref[...] / l_ref[...]).astype(BF16)
    o_ref[0] = jnp.dot(ctx, wuv_ref[...], preferred_element_type=F32)


def mla_decode(q, lat_new, kr_new, cache_lat, cache_kr, page_table, layer, w_uk, w_uv, g_kn):
    bd, t, nh, _, _ = q.shape
    assert nh * t == DECODE_ROWS and t == 8
    n_pages = page_table.shape[1]
    n_blocks = n_pages // DECODE_PAGES
    qh = jnp.transpose(q[:, :, :, 0, :], (0, 2, 1, 3))
    qnope = qh[..., :MLA_NOPE] * g_kn.astype(F32)
    zero = jnp.zeros((bd, t, MLA_NOPE), F32)
    qn = jnp.concatenate([jnp.concatenate([zero] * h + [qnope[:, h]] + [zero] * (nh - 1 - h), axis=-1)
                          for h in range(nh)], axis=1).astype(BF16)
    qr = qh[..., MLA_NOPE:].reshape(bd, nh * t, MLA_ROPE).astype(BF16)
    pad_rows = ((0, 0), (0, PAGE_SIZE - t), (0, 0))
    latn = jnp.pad(lat_new, pad_rows).astype(BF16)
    krn = jnp.pad(kr_new, pad_rows).astype(BF16)
    causal = jnp.tril(jnp.ones((t, t), bool))
    bn = jnp.broadcast_to(jnp.where(causal, 0.0, -jnp.inf)[None, None], (bd, nh, t, t)).reshape(bd, nh * t, t)
    bn = jnp.pad(bn.astype(F32), ((0, 0), (0, 0), (0, PAGE_SIZE - t)), constant_values=-jnp.inf)
    wuk = w_uk.reshape(MLA_KV_RANK, nh * MLA_NOPE).astype(BF16)
    wuv = w_uv.reshape(MLA_KV_RANK, nh * MLA_V).astype(BF16)
    seg = (jnp.arange(8)[:, None] == (jnp.arange(nh * MLA_NOPE) // MLA_NOPE)[None, :]).astype(BF16)
    row_spec = lambda shape: pl.BlockSpec((1,) + shape, lambda b, pt: (b, 0, 0))
    full = lambda shape: pl.BlockSpec(shape, lambda b, pt: (0, 0))
    out = pl.pallas_call(
        functools.partial(_mla_decode_kernel, n_blocks=n_blocks, scale=(MLA_NOPE + MLA_ROPE) ** -0.5,
                          layer=layer),
        grid_spec=pltpu.PrefetchScalarGridSpec(
            num_scalar_prefetch=1,
            grid=(bd,),
            in_specs=[row_spec((DECODE_ROWS, nh * MLA_NOPE)), row_spec((DECODE_ROWS, MLA_ROPE)),
                      row_spec((PAGE_SIZE, MLA_KV_RANK)), row_spec((PAGE_SIZE, MLA_ROPE)),
                      row_spec((DECODE_ROWS, PAGE_SIZE)),
                      full((MLA_KV_RANK, nh * MLA_NOPE)), full((8, nh * MLA_NOPE)), full((MLA_KV_RANK, nh * MLA_V)),
                      pl.BlockSpec(memory_space=pl.ANY), pl.BlockSpec(memory_space=pl.ANY)],
            out_specs=row_spec((DECODE_ROWS, nh * MLA_V)),
            scratch_shapes=[pltpu.VMEM((2, DECODE_PAGES, PAGE_SIZE, MLA_KV_RANK), F32),
                            pltpu.VMEM((2, DECODE_PAGES, PAGE_SIZE, MLA_ROPE), F32),
                            pltpu.SemaphoreType.DMA((2, 2)),
                            pltpu.VMEM((DECODE_ROWS, 1), F32), pltpu.VMEM((DECODE_ROWS, 1), F32),
                            pltpu.VMEM((DECODE_ROWS, MLA_KV_RANK), F32)]),
        out_shape=jax.ShapeDtypeStruct((bd, DECODE_ROWS, nh * MLA_V), F32),
        compiler_params=pltpu.CompilerParams(dimension_semantics=("arbitrary",),
                                             vmem_limit_bytes=V7X_VMEM_LIMIT),
        name="mla_decode",
    )(page_table, qn, qr, latn, krn, bn, wuk, seg, wuv, cache_lat, cache_kr)
    o = out.reshape(bd, nh, t, nh, MLA_V)
    o = jnp.stack([o[:, h, :, h] for h in range(nh)], axis=1)
    return jnp.transpose(o, (0, 2, 1, 3))[:, :, :, None, :]


def _online_update(m, l, acc, s, v):
    m_new = jnp.maximum(m, s.max(-1))
    p = jnp.exp(s - m_new[..., None])
    corr = jnp.exp(m - m_new)
    acc = acc * corr[..., None] + jnp.einsum('bhgts,bshe->bhgte', p, v.astype(F32))
    return m_new, l * corr + p.sum(-1), acc


def decode_attention(q, k_new, v_new, lf_new, load_past, n_blocks):
    bd, t, n_kv, n_grp, dk = q.shape
    dv = v_new.shape[-1]
    scale = dk ** -0.5
    use_forget = lf_new is not None
    if use_forget:
        g_new = jnp.transpose(jnp.cumsum(lf_new, axis=1), (0, 2, 3, 1))
        g_q = g_new[..., :, None]

    def past_block(carry, j):
        m, l, acc = carry[:3]
        k, v, lf = load_past(j)
        s = jnp.einsum('bthgd,bshd->bhgts', q, k).astype(F32) * scale
        extra = ()
        if use_forget:
            suf = carry[3]
            tot = lf.sum(axis=1)
            s_past = suf[:, None] + tot[:, None] - jnp.cumsum(lf, axis=1)
            s = s + jnp.transpose(s_past, (0, 2, 3, 1))[:, :, :, None, :] + g_q
            extra = (suf + tot,)
        m, l, acc = _online_update(m, l, acc, s, v)
        return (m, l, acc) + extra, None

    init = (jnp.full((bd, n_kv, n_grp, t), -jnp.inf, F32), jnp.zeros((bd, n_kv, n_grp, t), F32),
            jnp.zeros((bd, n_kv, n_grp, t, dv), F32))
    if use_forget:
        init = init + (jnp.zeros((bd, n_kv, n_grp), F32),)
    carry, _ = lax.scan(past_block, init, jnp.arange(n_blocks), reverse=True)
    m, l, acc = carry[:3]
    s = jnp.einsum('bthgd,bshd->bhgts', q, k_new).astype(F32) * scale
    if use_forget:
        s = s + g_q - g_new[..., None, :]
    causal = jnp.tril(jnp.ones((t, t), bool))
    s = jnp.where(causal, s, -jnp.inf)
    m, l, acc = _online_update(m, l, acc, s, v_new)
    out = acc / l[..., None]
    return jnp.transpose(out, (0, 3, 1, 2, 4)).astype(v_new.dtype)


def mlstm_chunkwise(q, k, v, ig, lf, c0, n0, m0):
    bsz, seq, nh, dk = q.shape
    dv = v.shape[-1]
    ln = math.gcd(seq, MLSTM_CHUNK)
    nc = seq // ln
    q = q.astype(F32)
    k = k.astype(F32) * dk ** -0.5
    v = v.astype(F32)

    def chunks(a):
        return jnp.moveaxis(a.reshape((bsz, nc, ln) + a.shape[2:]), 1, 0)

    causal = jnp.tril(jnp.ones((ln, ln), bool))[None, :, :, None]

    def step(carry, inp):
        c, n, m = carry
        qc, kc, vc, ic, fc = inp
        b = jnp.cumsum(fc, axis=1)
        dmat = b[:, :, None, :] - b[:, None, :, :] + ic[:, None, :, :]
        dmat = jnp.where(causal, dmat, -jnp.inf)
        inter = b + m[:, None, :]
        mt = jnp.maximum(inter, dmat.max(axis=2))
        w = jnp.exp(dmat - mt[:, :, None, :])
        ei = jnp.exp(inter - mt)
        a = w * jnp.einsum('bthd,bshd->btsh', qc, kc)
        num = ei[..., None] * jnp.einsum('bhed,bthd->bthe', c, qc) + jnp.einsum('btsh,bshe->bthe', a, vc)
        den = ei * jnp.einsum('bhd,bthd->bth', n, qc) + a.sum(axis=2)
        h = num / jnp.maximum(jnp.abs(den), jnp.exp(-mt))[..., None]
        b_last = b[:, -1]
        wlog = b_last[:, None] - b + ic
        m_new = jnp.maximum(b_last + m, wlog.max(axis=1))
        ws = jnp.exp(wlog - m_new[:, None])
        decay = jnp.exp(b_last + m - m_new)
        c_new = decay[..., None, None] * c + jnp.einsum('bsh,bshe,bshd->bhed', ws, vc, kc)
        n_new = decay[..., None] * n + jnp.einsum('bsh,bshd->bhd', ws, kc)
        return (c_new, n_new, m_new), h

    init = (c0.astype(F32), n0.astype(F32), m0.astype(F32))
    (c1, n1, m1), hs = lax.scan(step, init, (chunks(q), chunks(k), chunks(v), chunks(ig), chunks(lf)))
    h = jnp.moveaxis(hs, 0, 1).reshape(bsz, seq, nh, dv)
    return h, c1, n1, m1


def _gelu(x):
    return 0.5 * x * (1.0 + lax.erf(x * (2.0 ** -0.5)))


S5_NCH = S5_GROUPS * S5_STATE
S5_SCAN_ROWS = 8


def _s5_scan_block(xr, xi, pr, pi, cr, ci, row):
    for dist in (1, 2, 4):
        ar = pr[dist - 1:dist]
        ai = pi[dist - 1:dist]
        sr = jnp.where(row >= dist, pltpu.roll(xr, dist, 0), 0.0)
        si = jnp.where(row >= dist, pltpu.roll(xi, dist, 0), 0.0)
        xr, xi = xr + ar * sr - ai * si, xi + ar * si + ai * sr
    return xr + pr * cr - pi * ci, xi + pr * ci + pi * cr


def _s5_readout(u, xr_ref, xi_ref, cre_ref, cim_ref, d_ref, wglu_ref, bglu_ref):
    y = (jnp.dot(xr_ref[...].astype(BF16), cre_ref[...], preferred_element_type=F32)
         - jnp.dot(xi_ref[...].astype(BF16), cim_ref[...], preferred_element_type=F32)
         + u * d_ref[...])
    y = _gelu(y)
    gate = jnp.dot(y.astype(BF16), wglu_ref[...], preferred_element_type=F32) + bglu_ref[...]
    return y * jax.nn.sigmoid(gate)


def _s5_project(u, bre_ref, bim_ref, xr_ref, xi_ref):
    xr_ref[...] = jnp.dot(u, bre_ref[...], preferred_element_type=F32, precision=lax.Precision.HIGHEST)
    xi_ref[...] = jnp.dot(u, bim_ref[...], preferred_element_type=F32, precision=lax.Precision.HIGHEST)


def _s5_seq_kernel(u_ref, bre_ref, bim_ref, cre_ref, cim_ref, d_ref, wglu_ref, bglu_ref, p_ref, x0_ref,
                   o_ref, st_ref, xr_ref, xi_ref, car_ref):
    ts = u_ref.shape[1]
    u = u_ref[0]
    _s5_project(u, bre_ref, bim_ref, xr_ref, xi_ref)

    @pl.when(pl.program_id(1) == 0)
    def _():
        car_ref[...] = x0_ref[0]

    pr = p_ref[0]
    pi = p_ref[1]
    row = lax.broadcasted_iota(jnp.int32, (S5_SCAN_ROWS, S5_NCH), 0)

    def blk(j, carry):
        cr, ci = carry
        r0 = pl.multiple_of(j * S5_SCAN_ROWS, S5_SCAN_ROWS)
        xr, xi = _s5_scan_block(xr_ref[pl.ds(r0, S5_SCAN_ROWS), :], xi_ref[pl.ds(r0, S5_SCAN_ROWS), :],
                                pr, pi, cr, ci, row)
        xr_ref[pl.ds(r0, S5_SCAN_ROWS), :] = xr
        xi_ref[pl.ds(r0, S5_SCAN_ROWS), :] = xi
        return xr[S5_SCAN_ROWS - 1:], xi[S5_SCAN_ROWS - 1:]

    cr, ci = lax.fori_loop(0, ts // S5_SCAN_ROWS, blk, (car_ref[0:1], car_ref[1:2]))
    car_ref[0:1] = cr
    car_ref[1:2] = ci
    st_ref[0] = car_ref[...]
    o_ref[0] = _s5_readout(u, xr_ref, xi_ref, cre_ref, cim_ref, d_ref, wglu_ref, bglu_ref)


def _s5_short_kernel(u_ref, bre_ref, bim_ref, cre_ref, cim_ref, d_ref, wglu_ref, bglu_ref, p_ref, x0_ref,
                     o_ref, st_ref, xr_ref, xi_ref):
    nseq = x0_ref.shape[0]
    u = u_ref[...]
    _s5_project(u, bre_ref, bim_ref, xr_ref, xi_ref)
    pr = p_ref[0]
    pi = p_ref[1]
    row = lax.broadcasted_iota(jnp.int32, (S5_SCAN_ROWS, S5_NCH), 0)

    def blk(j, carry):
        r0 = pl.multiple_of(j * S5_SCAN_ROWS, S5_SCAN_ROWS)
        x0 = x0_ref[j]
        xr, xi = _s5_scan_block(xr_ref[pl.ds(r0, S5_SCAN_ROWS), :], xi_ref[pl.ds(r0, S5_SCAN_ROWS), :],
                                pr, pi, x0[0:1], x0[1:2], row)
        xr_ref[pl.ds(r0, S5_SCAN_ROWS), :] = xr
        xi_ref[pl.ds(r0, S5_SCAN_ROWS), :] = xi
        st_ref[j, 0:1, :] = xr[S5_SCAN_ROWS - 1:]
        st_ref[j, 1:2, :] = xi[S5_SCAN_ROWS - 1:]
        return carry

    lax.fori_loop(0, nseq, blk, 0)
    o_ref[...] = _s5_readout(u, xr_ref, xi_ref, cre_ref, cim_ref, d_ref, wglu_ref, bglu_ref)


def _s5_tables(a_re, a_im, log_step, b_re, b_im, c_re, c_im):
    step = jnp.exp(log_step.astype(F32))[:, None]
    mag = jnp.exp(a_re * step)
    abr = mag * jnp.cos(a_im * step)
    abi = mag * jnp.sin(a_im * step)
    den = a_re * a_re + a_im * a_im
    cr = ((abr - 1.0) * a_re + abi * a_im) / den
    ci = (abi * a_re - (abr - 1.0) * a_im) / den
    bbr = cr[..., None] * b_re - ci[..., None] * b_im
    bbi = cr[..., None] * b_im + ci[..., None] * b_re
    eye = jnp.eye(S5_GROUPS, dtype=F32)
    bre = jnp.einsum('gpi,gh->gihp', bbr, eye).reshape(S5_CH, S5_NCH)
    bim = jnp.einsum('gpi,gh->gihp', bbi, eye).reshape(S5_CH, S5_NCH)
    cre = jnp.einsum('gop,gh->gpho', c_re, eye).reshape(S5_NCH, S5_CH).astype(BF16)
    cim = jnp.einsum('gop,gh->gpho', c_im, eye).reshape(S5_NCH, S5_CH).astype(BF16)
    ar = abr.reshape(1, S5_NCH)
    ai = abi.reshape(1, S5_NCH)
    pws = [(ar, ai)]
    for _ in range(S5_SCAN_ROWS - 1):
        qr, qi = pws[-1]
        pws.append((qr * ar - qi * ai, qr * ai + qi * ar))
    powers = jnp.stack([jnp.concatenate([p[0] for p in pws], axis=0),
                        jnp.concatenate([p[1] for p in pws], axis=0)], axis=0)
    return bre, bim, cre, cim, powers


def s5_layer(u, a_re, a_im, log_step, b_re, b_im, c_re, c_im, d, w_glu, b_glu, x0_re, x0_im):
    bsz, seq, _ = u.shape
    bre, bim, cre, cim, powers = _s5_tables(a_re, a_im, log_step, b_re, b_im, c_re, c_im)
    x0 = jnp.stack([x0_re.reshape(bsz, S5_NCH), x0_im.reshape(bsz, S5_NCH)], axis=1)
    d2 = d.astype(F32).reshape(1, S5_CH)
    bg = b_glu.astype(F32).reshape(1, S5_CH)
    wg = w_glu.astype(BF16)
    const = lambda *_: (0, 0)
    weight_specs = [pl.BlockSpec((S5_CH, S5_NCH), const), pl.BlockSpec((S5_CH, S5_NCH), const),
                    pl.BlockSpec((S5_NCH, S5_CH), const), pl.BlockSpec((S5_NCH, S5_CH), const),
                    pl.BlockSpec((1, S5_CH), const), pl.BlockSpec((S5_CH, S5_CH), const),
                    pl.BlockSpec((1, S5_CH), const),
                    pl.BlockSpec((2, S5_SCAN_ROWS, S5_NCH), lambda *_: (0, 0, 0))]
    weights = (bre, bim, cre, cim, d2, wg, bg, powers)
    if seq == S5_SCAN_ROWS:
        nseq = math.gcd(bsz, 64)
        rows = nseq * seq
        out, st = pl.pallas_call(
            _s5_short_kernel,
            grid=(bsz // nseq,),
            in_specs=[pl.BlockSpec((rows, S5_CH), lambda i: (i, 0))] + weight_specs
                     + [pl.BlockSpec((nseq, 2, S5_NCH), lambda i: (i, 0, 0))],
            out_specs=[pl.BlockSpec((rows, S5_CH), lambda i: (i, 0)),
                       pl.BlockSpec((nseq, 2, S5_NCH), lambda i: (i, 0, 0))],
            out_shape=[jax.ShapeDtypeStruct((bsz * seq, S5_CH), F32),
                       jax.ShapeDtypeStruct((bsz, 2, S5_NCH), F32)],
            scratch_shapes=[pltpu.VMEM((rows, S5_NCH), F32), pltpu.VMEM((rows, S5_NCH), F32)],
            compiler_params=pltpu.CompilerParams(dimension_semantics=("arbitrary",),
                                                 vmem_limit_bytes=V7X_VMEM_LIMIT),
            name="s5_short",
        )(u.reshape(bsz * seq, S5_CH), *weights, x0)
        out = out.reshape(bsz, seq, S5_CH)
    else:
        ts = math.gcd(seq, 512)
        out, st = pl.pallas_call(
            _s5_seq_kernel,
            grid=(bsz, seq // ts),
            in_specs=[pl.BlockSpec((1, ts, S5_CH), lambda b, s: (b, s, 0))] + weight_specs
                     + [pl.BlockSpec((1, 2, S5_NCH), lambda b, s: (b, 0, 0))],
            out_specs=[pl.BlockSpec((1, ts, S5_CH), lambda b, s: (b, s, 0)),
                       pl.BlockSpec((1, 2, S5_NCH), lambda b, s: (b, 0, 0))],
            out_shape=[jax.ShapeDtypeStruct((bsz, seq, S5_CH), F32),
                       jax.ShapeDtypeStruct((bsz, 2, S5_NCH), F32)],
            scratch_shapes=[pltpu.VMEM((ts, S5_NCH), F32), pltpu.VMEM((ts, S5_NCH), F32),
                            pltpu.VMEM((2, S5_NCH), F32)],
            compiler_params=pltpu.CompilerParams(dimension_semantics=("arbitrary", "arbitrary"),
                                                 vmem_limit_bytes=V7X_VMEM_LIMIT),
            name="s5_seq",
        )(u, *weights, x0)
    s_re = st[:, 0].reshape(bsz, S5_GROUPS, S5_STATE)
    s_im = st[:, 1].reshape(bsz, S5_GROUPS, S5_STATE)
    return out, s_re, s_im


def mla_keys(lat, krope, w_uk, w_uv, g_kn):
    k_nope = rmsnorm(jnp.einsum('bsr,rhd->bshd', lat, w_uk), g_kn)
    v = jnp.einsum('bsr,rhd->bshd', lat, w_uv)
    kr = jnp.broadcast_to(krope[:, :, None, :], k_nope.shape[:3] + (MLA_ROPE,)).astype(k_nope.dtype)
    return jnp.concatenate([k_nope, kr], axis=-1), v


PEER_HALF = PEER_KEY_DIM // 2
PEER_QW = PEER_HEADS * PEER_KEY_DIM
PEER_ROUTE_TOKENS = 256
PEER_TOKENS = 512
PEER_KEY_ROWS = 8
PEER_EXPERT_BLOCK = PEER_KEY_ROWS * PEER_NKEYS
PEER_ROW_BLOCK = 32


def _extract_top(xs, rows_f):
    slot = lax.broadcasted_iota(jnp.int32, (PEER_TOPK, V7X_LANES), 0)
    nrows = float(xs[0].shape[0])

    def body(r, carry):
        out = []
        for x, sv in carry:
            m = jnp.max(x, axis=0, keepdims=True)
            first = jnp.min(jnp.where(x == m, rows_f, nrows), axis=0, keepdims=True)
            out.append((jnp.where(rows_f == first, -jnp.inf, x), jnp.where(slot == r, m, sv)))
        return tuple(out)

    init = tuple((x, jnp.zeros((PEER_TOPK, V7X_LANES), F32)) for x in xs)
    return [sv for _, sv in lax.fori_loop(0, PEER_TOPK, body, init)]


PEER_PAIR_PIECES = tuple((b, PEER_TOPK // (b + 1)) for b in range(1, 8))
PEER_PAIR_ROWS = PEER_TOPK + 8 * len(PEER_PAIR_PIECES) + 8


def _pair_candidates(sv0, sv1):
    row8 = lax.broadcasted_iota(jnp.int32, (8, V7X_LANES), 0)
    pieces = [sv0 + sv1[0:1]]
    for b, n_valid in PEER_PAIR_PIECES:
        pieces.append(jnp.where(row8 < n_valid, sv0[0:8] + sv1[b:b + 1], -jnp.inf))
    pieces.append(sv0[0:1] + sv1[8:16])
    return jnp.concatenate(pieces, axis=0)


def _peer_route_kernel(h_ref, g_ref, wqt_ref, sk_ref, xb_ref, s0_ref, scl_ref, s1_ref, e1_ref, tau_ref,
                       st_ref):
    tr = h_ref.shape[0]
    hf = h_ref[...]
    c = hf * lax.rsqrt(jnp.mean(hf * hf, axis=-1, keepdims=True) + EPS) * g_ref[...]
    cb = c.astype(BF16)
    xb_ref[...] = cb
    qt = lax.dot_general(wqt_ref[...], cb, (((1,), (1,)), ((), ())), preferred_element_type=F32)
    for hc in range(2 * PEER_HEADS):
        st_ref[hc] = jnp.dot(sk_ref[hc], qt[hc * PEER_HALF:(hc + 1) * PEER_HALF].astype(BF16),
                             preferred_element_type=F32)

    rows_key = lax.broadcasted_iota(jnp.int32, (PEER_NKEYS, V7X_LANES), 0).astype(F32)
    rows_pair = lax.broadcasted_iota(jnp.int32, (PEER_PAIR_ROWS, V7X_LANES), 0).astype(F32)
    lane_groups = [slice(lg * V7X_LANES, (lg + 1) * V7X_LANES) for lg in range(tr // V7X_LANES)]

    def head(h, carry):
        tops = []
        for lanes in lane_groups:
            s0 = st_ref[2 * h, :, lanes]
            s1 = st_ref[2 * h + 1, :, lanes]
            tops.append(_extract_top([s0, s1], rows_key))
        bests = _extract_top([_pair_candidates(sv0, sv1) for sv0, sv1 in tops], rows_pair)
        for lanes, (sv0, sv1), best in zip(lane_groups, tops, bests):
            s0 = st_ref[2 * h, :, lanes]
            s1 = st_ref[2 * h + 1, :, lanes]
            z = jnp.sum(jnp.exp(best - best[0:1]), axis=0, keepdims=True)
            s0_ref[h, :, lanes] = s0
            scl_ref[h, :, lanes] = jnp.exp(s0 - sv0[0:1]) / z
            s1_ref[h, :, lanes] = s1
            e1_ref[h, :, lanes] = jnp.exp(s1 - sv1[0:1])
            tau_ref[h, :, lanes] = best[PEER_TOPK - 1:PEER_TOPK]
        return carry

    lax.fori_loop(0, PEER_HEADS, head, 0)


def _peer_expert_kernel(x_ref, u_ref, vt_ref, s0_ref, scl_ref, s1_ref, e1_ref, tau_ref, o_ref,
                        acc_ref, a_ref, s_ref):
    ei = pl.program_id(1)
    tt = x_ref.shape[0]
    key_rows = u_ref.shape[0] // PEER_NKEYS

    @pl.when(ei == 0)
    def _():
        acc_ref[...] = jnp.zeros_like(acc_ref)

    s_ref[...] = lax.dot_general(u_ref[...], x_ref[...], (((1,), (1,)), ((), ())), preferred_element_type=F32)

    assert key_rows == PEER_KEY_ROWS
    first_keys = pl.ds(pl.multiple_of(ei * PEER_KEY_ROWS, PEER_KEY_ROWS), PEER_KEY_ROWS)
    for lg in range(tt // V7X_LANES):
        lanes = slice(lg * V7X_LANES, (lg + 1) * V7X_LANES)
        s0rows = [s0_ref[h, first_keys, lanes] for h in range(PEER_HEADS)]
        sclrows = [scl_ref[h, first_keys, lanes] for h in range(PEER_HEADS)]
        taurow = [tau_ref[h, :, lanes] for h in range(PEER_HEADS)]
        for r in range(key_rows):
            s0row = [v[r:r + 1] for v in s0rows]
            sclrow = [v[r:r + 1] for v in sclrows]
            for jb in range(PEER_NKEYS // PEER_ROW_BLOCK):
                rs = slice(jb * PEER_ROW_BLOCK, (jb + 1) * PEER_ROW_BLOCK)
                es = slice(r * PEER_NKEYS + jb * PEER_ROW_BLOCK, r * PEER_NKEYS + (jb + 1) * PEER_ROW_BLOCK)
                gate = jnp.zeros((PEER_ROW_BLOCK, V7X_LANES), F32)
                for h in range(PEER_HEADS):
                    pair = s1_ref[h, rs, lanes] + s0row[h]
                    gate = gate + jnp.where(pair >= taurow[h], e1_ref[h, rs, lanes], 0.0) * sclrow[h]
                a_ref[es, lanes] = (gate * _gelu(s_ref[es, lanes])).astype(BF16)

    acc_ref[...] += jnp.dot(vt_ref[...], a_ref[...], preferred_element_type=F32)

    @pl.when(ei == pl.num_programs(1) - 1)
    def _():
        o_ref[...] = acc_ref[...].T


def _peer_tables_kernel(u_ref, v_ref, ub_ref, vt_ref):
    ub_ref[...] = u_ref[...].astype(BF16)
    vt_ref[...] = v_ref[...].T.astype(BF16)


def peer_tables(u_tab, v_tab):
    nexp, dm = u_tab.shape
    te = math.gcd(nexp, 512)
    return pl.pallas_call(
        _peer_tables_kernel,
        grid=(nexp // te,),
        in_specs=[pl.BlockSpec((te, dm), lambda i: (i, 0)), pl.BlockSpec((te, dm), lambda i: (i, 0))],
        out_specs=[pl.BlockSpec((te, dm), lambda i: (i, 0)), pl.BlockSpec((dm, te), lambda i: (0, i))],
        out_shape=[jax.ShapeDtypeStruct((nexp, dm), BF16), jax.ShapeDtypeStruct((dm, nexp), BF16)],
        compiler_params=pltpu.CompilerParams(dimension_semantics=("arbitrary",),
                                             vmem_limit_bytes=V7X_VMEM_LIMIT),
        name="peer_tables",
    )(u_tab, v_tab)


def peer_ffn(hres, g_ffn, w_q, subkeys, ub, vt):
    bsz, seq, dm = hres.shape
    ntok = bsz * seq
    nexp = ub.shape[0]
    tr = math.gcd(ntok, PEER_ROUTE_TOKENS)
    tt = math.gcd(ntok, PEER_TOKENS)
    wqt = w_q.T.astype(BF16)
    sk = subkeys.reshape(2 * PEER_HEADS, PEER_NKEYS, PEER_HALF).astype(BF16)
    key_shape = (PEER_HEADS, PEER_NKEYS, ntok)
    key_block = lambda t: pl.BlockSpec((PEER_HEADS, PEER_NKEYS, t), lambda i, *_: (0, 0, i))
    tau_block = lambda t: pl.BlockSpec((PEER_HEADS, 1, t), lambda i, *_: (0, 0, i))
    xb, s0t, sclt, s1t, e1t, taut = pl.pallas_call(
        _peer_route_kernel,
        grid=(ntok // tr,),
        in_specs=[pl.BlockSpec((tr, dm), lambda i: (i, 0)),
                  pl.BlockSpec((1, dm), lambda i: (0, 0)),
                  pl.BlockSpec((PEER_QW, dm), lambda i: (0, 0)),
                  pl.BlockSpec((2 * PEER_HEADS, PEER_NKEYS, PEER_HALF), lambda i: (0, 0, 0))],
        out_specs=[pl.BlockSpec((tr, dm), lambda i: (i, 0)),
                   key_block(tr), key_block(tr), key_block(tr), key_block(tr), tau_block(tr)],
        out_shape=[jax.ShapeDtypeStruct((ntok, dm), BF16)] + [jax.ShapeDtypeStruct(key_shape, F32)] * 4
                  + [jax.ShapeDtypeStruct((PEER_HEADS, 1, ntok), F32)],
        scratch_shapes=[pltpu.VMEM((2 * PEER_HEADS, PEER_NKEYS, tr), F32)],
        compiler_params=pltpu.CompilerParams(dimension_semantics=("arbitrary",),
                                             vmem_limit_bytes=V7X_VMEM_LIMIT),
        name="peer_route",
    )(hres.reshape(ntok, dm), g_ffn.astype(F32).reshape(1, dm), wqt, sk)

    eb = PEER_EXPERT_BLOCK
    out = pl.pallas_call(
        _peer_expert_kernel,
        grid=(ntok // tt, nexp // eb),
        in_specs=[pl.BlockSpec((tt, dm), lambda t, e: (t, 0)),
                  pl.BlockSpec((eb, dm), lambda t, e: (e, 0)),
                  pl.BlockSpec((dm, eb), lambda t, e: (0, e)),
                  key_block(tt), key_block(tt), key_block(tt), key_block(tt), tau_block(tt)],
        out_specs=pl.BlockSpec((tt, dm), lambda t, e: (t, 0)),
        out_shape=jax.ShapeDtypeStruct((ntok, dm), F32),
        scratch_shapes=[pltpu.VMEM((dm, tt), F32), pltpu.VMEM((eb, tt), BF16), pltpu.VMEM((eb, tt), F32)],
        compiler_params=pltpu.CompilerParams(dimension_semantics=("arbitrary", "arbitrary"),
                                             vmem_limit_bytes=V7X_VMEM_LIMIT),
        name="peer_experts",
    )(xb, ub, vt, s0t, sclt, s1t, e1t, taut)
    return out.reshape(bsz, seq, dm)


def mixing_sublayer(a, pos, lp, past):
    bsz, seq, _ = a.shape
    z = matmul3(a, lp['w_in'])
    points = [int(t) for t in np.cumsum(SPLIT_SIZES)[:-1]]
    (fq, fk, fv, ff, mq, mk, mv, mi, mf, mo, su, cq, ckv, kr) = jnp.split(z, points, axis=-1)

    q_f = rmsnorm(fq.reshape(bsz, seq, FOX_KV_HEADS, FOX_GROUP, HEAD_DIM), lp['g_fox_q'])
    k_f = rmsnorm(fk.reshape(bsz, seq, FOX_KV_HEADS, HEAD_DIM), lp['g_fox_k'])
    v_f = fv.reshape(bsz, seq, FOX_KV_HEADS, HEAD_DIM)
    lf_f = jax.nn.log_sigmoid(ff.astype(F32) + lp['b_fox_f'].astype(F32))
    lf_g = lf_f.reshape(bsz, seq, FOX_KV_HEADS, FOX_GROUP)

    c_q = rmsnorm(cq, lp['g_mla_cq'])
    q_m = jnp.einsum('bsr,rhd->bshd', c_q, lp['w_mla_uq'])
    q_m = jnp.concatenate([rmsnorm(q_m[..., :MLA_NOPE], lp['g_mla_qn']),
                           rope(rmsnorm(q_m[..., MLA_NOPE:], lp['g_mla_qr']), pos)], axis=-1)[:, :, :, None, :]
    lat = rmsnorm(ckv, lp['g_mla_ckv'])
    krope = rope(rmsnorm(kr, lp['g_mla_kr'])[:, :, None, :], pos)[:, :, 0, :]
    k_m, v_m = mla_keys(lat, krope, lp['w_mla_uk'], lp['w_mla_uv'], lp['g_mla_kn'])

    if past is None:
        o_f = prompt_attention(q_f, k_f, v_f, jnp.cumsum(lf_g, axis=1))
        o_m = prompt_attention(q_m, k_m, v_m, None)
        c0 = jnp.zeros((bsz, MLSTM_HEADS, HEAD_DIM, HEAD_DIM), F32)
        n0 = jnp.zeros((bsz, MLSTM_HEADS, HEAD_DIM), F32)
        m0 = jnp.zeros((bsz, MLSTM_HEADS), F32)
        x0_re = jnp.zeros((bsz, S5_GROUPS, S5_STATE), F32)
        x0_im = jnp.zeros((bsz, S5_GROUPS, S5_STATE), F32)
    else:
        o_f = fox_decode(q_f, k_f, v_f, lf_g, past['fox_k'], past['fox_v'], past['fox_lft'], past['page_table'],
                         past['layer'])
        o_m = mla_decode(q_m, lat, krope, past['mla_lat'], past['mla_kr'], past['page_table'], past['layer'],
                         lp['w_mla_uk'], lp['w_mla_uv'], lp['g_mla_kn'])
        c0, n0, m0, x0_re, x0_im = past['rec']

    ig = mi.astype(F32) + lp['b_mlstm_i'].astype(F32)
    lf_l = jax.nn.log_sigmoid(mf.astype(F32) + lp['b_mlstm_f'].astype(F32))
    h_l, c1, n1, m1 = mlstm_chunkwise(mq.reshape(bsz, seq, MLSTM_HEADS, HEAD_DIM),
                                      mk.reshape(bsz, seq, MLSTM_HEADS, HEAD_DIM),
                                      mv.reshape(bsz, seq, MLSTM_HEADS, HEAD_DIM), ig, lf_l, c0, n0, m0)
    o_l = rmsnorm(h_l, lp['g_mlstm_h']).astype(a.dtype).reshape(bsz, seq, MLSTM_W) * jax.nn.sigmoid(mo)

    o_s, s_re, s_im = s5_layer(su, lp['s5_a_re'], lp['s5_a_im'], lp['s5_log_step'], lp['s5_b_re'], lp['s5_b_im'],
                               lp['s5_c_re'], lp['s5_c_im'], lp['s5_d'], lp['w_glu'], lp['b_glu'], x0_re, x0_im)

    mix = jnp.concatenate([o_f.reshape(bsz, seq, -1), o_l, o_s, o_m.reshape(bsz, seq, -1)], axis=-1)
    out = matmul3(mix, lp['w_out'])
    return out, (k_f, v_f, lf_f, lat, krope, c1, n1, m1, s_re, s_im)


def run_trunk(x, p, pos, layer_params, pasts):
    h = x
    states = []
    for i in range(DEPTH):
        lp = layer_params[i]
        mix, st = mixing_sublayer(rmsnorm(h, lp['g_mix']), pos, lp, pasts[i])
        h = h + mix
        h = h + peer_ffn(h, lp['g_ffn'], lp['w_peer_q'], lp['peer_subkeys'], lp['peer_ub'], lp['peer_vt'])
        gate = jax.nn.sigmoid(matmul3(rmsnorm(h, lp['g_ple']), lp['w_ple_gate']))
        h = h + matmul3(p[i], lp['w_ple']) * gate
        states.append(st)
    stacked = [jnp.stack([st[j] for st in states], axis=0) for j in range(len(states[0]))]
    return h, stacked


def kernel(x_prompt, x_sample, cache_fox_k, cache_fox_v, cache_fox_logf, cache_mla_latent, cache_mla_krope,
           state_mlstm_C, state_mlstm_n, state_mlstm_m, state_s5_re, state_s5_im, page_table, p_prompt, p_sample,
           g_mix, w_in, g_fox_q, g_fox_k, b_fox_f, b_mlstm_i, b_mlstm_f, g_mlstm_h,
           s5_a_re, s5_a_im, s5_log_step, s5_b_re, s5_b_im, s5_c_re, s5_c_im, s5_d, w_glu, b_glu,
           g_mla_cq, w_mla_uq, g_mla_qn, g_mla_qr, g_mla_ckv, g_mla_kr, w_mla_uk, w_mla_uv, g_mla_kn,
           w_out, g_ffn, w_peer_q, peer_subkeys, peer_u, peer_v, g_ple, w_ple_gate, w_ple):
    names = ('g_mix', 'w_in', 'g_fox_q', 'g_fox_k', 'b_fox_f', 'b_mlstm_i', 'b_mlstm_f', 'g_mlstm_h',
             's5_a_re', 's5_a_im', 's5_log_step', 's5_b_re', 's5_b_im', 's5_c_re', 's5_c_im', 's5_d', 'w_glu',
             'b_glu', 'g_mla_cq', 'w_mla_uq', 'g_mla_qn', 'g_mla_qr', 'g_mla_ckv', 'g_mla_kr', 'w_mla_uk',
             'w_mla_uv', 'g_mla_kn', 'w_out', 'g_ffn', 'w_peer_q', 'peer_subkeys', 'peer_u', 'peer_v', 'g_ple',
             'w_ple_gate', 'w_ple')
    vals = (g_mix, w_in, g_fox_q, g_fox_k, b_fox_f, b_mlstm_i, b_mlstm_f, g_mlstm_h,
            s5_a_re, s5_a_im, s5_log_step, s5_b_re, s5_b_im, s5_c_re, s5_c_im, s5_d, w_glu,
            b_glu, g_mla_cq, w_mla_uq, g_mla_qn, g_mla_qr, g_mla_ckv, g_mla_kr, w_mla_uk,
            w_mla_uv, g_mla_kn, w_out, g_ffn, w_peer_q, peer_subkeys, peer_u, peer_v, g_ple,
            w_ple_gate, w_ple)
    layer_params = [{n: v[i] for n, v in zip(names, vals)} for i in range(DEPTH)]
    for lp in layer_params:
        lp['peer_ub'], lp['peer_vt'] = peer_tables(lp['peer_u'], lp['peer_v'])

    past_len = page_table.shape[1] * PAGE_SIZE
    cache_lft = jnp.swapaxes(cache_fox_logf, 2, 3)

    def make_past(i):
        return {'fox_k': cache_fox_k, 'fox_v': cache_fox_v, 'fox_lft': cache_lft, 'layer': i,
                'mla_lat': cache_mla_latent, 'mla_kr': cache_mla_krope, 'page_table': page_table,
                'rec': (state_mlstm_C[i], state_mlstm_n[i], state_mlstm_m[i], state_s5_re[i], state_s5_im[i])}

    pasts = [make_past(i) for i in range(DEPTH)]
    pos_s = past_len + jnp.arange(x_sample.shape[1], dtype=jnp.int32)
    y_sample, st_s = run_trunk(x_sample, p_sample, pos_s, layer_params, pasts)

    pos_p = jnp.arange(x_prompt.shape[1], dtype=jnp.int32)
    y_prompt, st_p = run_trunk(x_prompt, p_prompt, pos_p, layer_params, [None] * DEPTH)

    (pk, pv, plf, plat, pkr, pc, pn, pm, pre, pim) = st_p
    (sk, sv, slf, slat, skr, sc, sn, sm, sre, sim) = st_s
    return (y_prompt, y_sample, pk, pv, plf, plat, pkr, pc, pn, pm, pre, pim,
            sk, sv, slf, slat, skr, sc, sn, sm, sre, sim)
```

```python
import functools
import math

import jax
import jax.numpy as jnp
import numpy as np
from jax import lax
from jax.experimental import pallas as pl
from jax.experimental.pallas import tpu as pltpu

D_MODEL = 1024
DEPTH = 2
PAGE_SIZE = 128
HEAD_DIM = 64
GROUP_WIDTH = D_MODEL // 4
FOX_HEADS = GROUP_WIDTH // HEAD_DIM
FOX_KV_HEADS = FOX_HEADS // 2
FOX_GROUP = FOX_HEADS // FOX_KV_HEADS
MLSTM_HEADS = GROUP_WIDTH // HEAD_DIM
MLSTM_W = MLSTM_HEADS * HEAD_DIM
MLSTM_CHUNK = 64
S5_CH = GROUP_WIDTH
S5_GROUP_SIZE = 16
S5_GROUPS = S5_CH // S5_GROUP_SIZE
S5_STATE = 64
MLA_HEADS = GROUP_WIDTH // HEAD_DIM
MLA_NOPE = HEAD_DIM
MLA_ROPE = HEAD_DIM // 2
MLA_V = HEAD_DIM
MLA_Q_RANK = D_MODEL // 8
MLA_KV_RANK = D_MODEL // 8
ROPE_THETA = 10000.0
PEER_HEADS = 8
PEER_NKEYS = 128
PEER_KEY_DIM = 256
PEER_TOPK = 16
PEER_TOKEN_BLOCK = 256
Q_BLOCK = 128
MAX_KEY_BLOCK_PAGES = 8
EPS = 1e-6
F32 = jnp.float32
BF16 = jnp.bfloat16

SPLIT_SIZES = (FOX_HEADS * HEAD_DIM, FOX_KV_HEADS * HEAD_DIM, FOX_KV_HEADS * HEAD_DIM, FOX_HEADS,
               MLSTM_W, MLSTM_W, MLSTM_W, MLSTM_HEADS, MLSTM_HEADS, MLSTM_W,
               S5_CH,
               MLA_Q_RANK, MLA_KV_RANK, MLA_ROPE)

V7X_LANES = 128
V7X_VMEM_LIMIT = 48 * 1024 * 1024


def _mm_kernel(x_ref, w_ref, o_ref):
    o_ref[...] = jnp.dot(x_ref[...].astype(BF16), w_ref[...], preferred_element_type=F32)


def _row_tile(n):
    for t in (512, 256, 128, 64, 32, 16, 8):
        if n % t == 0:
            return t
    raise ValueError(f"row count {n} is not a multiple of 8")


def matmul(x, w):
    n, k = x.shape
    m = w.shape[1]
    m_pad = -(-m // V7X_LANES) * V7X_LANES
    wb = w.astype(BF16)
    if m_pad != m:
        wb = jnp.pad(wb, ((0, 0), (0, m_pad - m)))
    tm = _row_tile(n)
    out = pl.pallas_call(
        _mm_kernel,
        grid=(n // tm,),
        in_specs=[pl.BlockSpec((tm, k), lambda i: (i, 0)),
                  pl.BlockSpec((k, m_pad), lambda i: (0, 0))],
        out_specs=pl.BlockSpec((tm, m_pad), lambda i: (i, 0)),
        out_shape=jax.ShapeDtypeStruct((n, m_pad), F32),
        compiler_params=pltpu.CompilerParams(dimension_semantics=("arbitrary",),
                                             vmem_limit_bytes=V7X_VMEM_LIMIT),
        name="matmul",
    )(x, wb)
    return out[:, :m] if m_pad != m else out


def matmul3(x, w):
    b, s, k = x.shape
    return matmul(x.reshape(b * s, k), w).reshape(b, s, w.shape[1])


def rmsnorm(x, g):
    xf = x.astype(F32)
    y = xf * lax.rsqrt(jnp.mean(xf * xf, axis=-1, keepdims=True) + EPS)
    return (y * g.astype(F32)).astype(x.dtype)


def rope(x, pos):
    half = x.shape[-1] // 2
    inv_freq = ROPE_THETA ** (-jnp.arange(half, dtype=F32) / half)
    ang = pos.astype(F32)[:, None] * inv_freq[None, :]
    cos = jnp.cos(ang)[None, :, None, :]
    sin = jnp.sin(ang)[None, :, None, :]
    xf = x.astype(F32)
    x1, x2 = xf[..., :half], xf[..., half:]
    return jnp.concatenate([x1 * cos - x2 * sin, x2 * cos + x1 * sin], axis=-1).astype(x.dtype)


ATTN_BLOCK = 512


def _flash_kernel(q_ref, k_ref, v_ref, fq_ref, fk_ref, o_ref, m_ref, l_ref, acc_ref, *, scale, use_forget):
    qi = pl.program_id(2)
    ki = pl.program_id(3)

    @pl.when(ki == 0)
    def _():
        m_ref[...] = jnp.full_like(m_ref, -jnp.inf)
        l_ref[...] = jnp.zeros_like(l_ref)
        acc_ref[...] = jnp.zeros_like(acc_ref)

    def step(diagonal):
        s = lax.dot_general(q_ref[0, 0], k_ref[0, 0], (((1,), (1,)), ((), ())),
                            preferred_element_type=F32) * scale
        if use_forget:
            s = s + fq_ref[0, 0] - fk_ref[0, 0]
        if diagonal:
            row = lax.broadcasted_iota(jnp.int32, s.shape, 0)
            col = lax.broadcasted_iota(jnp.int32, s.shape, 1)
            s = jnp.where(row >= col, s, -jnp.inf)
        m_prev = m_ref[...]
        m_new = jnp.maximum(m_prev, jnp.max(s, axis=-1, keepdims=True))
        p = jnp.exp(s - m_new)
        corr = jnp.exp(m_prev - m_new)
        l_ref[...] = corr * l_ref[...] + jnp.sum(p, axis=-1, keepdims=True)
        acc_ref[...] = corr * acc_ref[...] + jnp.dot(p.astype(BF16), v_ref[0, 0], preferred_element_type=F32)
        m_ref[...] = m_new

    @pl.when(ki < qi)
    def _():
        step(False)

    @pl.when(ki == qi)
    def _():
        step(True)
        o_ref[0, 0] = acc_ref[...] / l_ref[...]


def prompt_attention(q, k, v, fcum):
    bsz, seq, n_kv, n_grp, dk = q.shape
    dv = v.shape[-1]
    nh = n_kv * n_grp
    tb = math.gcd(seq, ATTN_BLOCK)
    nb = seq // tb
    qh = jnp.transpose(q.reshape(bsz, seq, nh, dk), (0, 2, 1, 3)).astype(BF16)
    kh = jnp.transpose(k, (0, 2, 1, 3)).astype(BF16)
    vh = jnp.transpose(v, (0, 2, 1, 3)).astype(BF16)
    use_forget = fcum is not None
    if use_forget:
        ft = jnp.transpose(fcum.reshape(bsz, seq, nh), (0, 2, 1)).astype(F32)
    else:
        ft = jnp.zeros((bsz, nh, seq), F32)
    fq = ft[..., None]
    fk = ft[:, :, None, :]
    out = pl.pallas_call(
        functools.partial(_flash_kernel, scale=dk ** -0.5, use_forget=use_forget),
        grid=(bsz, nh, nb, nb),
        in_specs=[pl.BlockSpec((1, 1, tb, dk), lambda b, h, i, j: (b, h, i, 0)),
                  pl.BlockSpec((1, 1, tb, dk), lambda b, h, i, j: (b, h // n_grp, jnp.minimum(i, j), 0)),
                  pl.BlockSpec((1, 1, tb, dv), lambda b, h, i, j: (b, h // n_grp, jnp.minimum(i, j), 0)),
                  pl.BlockSpec((1, 1, tb, 1), lambda b, h, i, j: (b, h, i, 0)),
                  pl.BlockSpec((1, 1, 1, tb), lambda b, h, i, j: (b, h, 0, jnp.minimum(i, j)))],
        out_specs=pl.BlockSpec((1, 1, tb, dv), lambda b, h, i, j: (b, h, i, 0)),
        out_shape=jax.ShapeDtypeStruct((bsz, nh, seq, dv), F32),
        scratch_shapes=[pltpu.VMEM((tb, 1), F32), pltpu.VMEM((tb, 1), F32), pltpu.VMEM((tb, dv), F32)],
        compiler_params=pltpu.CompilerParams(
            dimension_semantics=("arbitrary", "arbitrary", "arbitrary", "arbitrary"),
            vmem_limit_bytes=V7X_VMEM_LIMIT),
        name="flash_attention",
    )(qh, kh, vh, fq, fk)
    return jnp.transpose(out, (0, 2, 1, 3)).reshape(bsz, seq, n_kv, n_grp, dv)


DECODE_PAGES = 8
DECODE_ROWS = MLA_HEADS * 8


def _page_copies(pt_ref, layer, b, blk, slot, srcs, bufs, sems):
    out = []
    for p in range(DECODE_PAGES):
        page = pt_ref[b, blk * DECODE_PAGES + p]
        for src, buf, sem in zip(srcs, bufs, sems):
            out.append(pltpu.make_async_copy(src.at[layer, page], buf.at[slot, p], sem.at[slot]))
    return out


def _head_rows(x4):
    n = x4.shape[1]
    return jnp.concatenate([jnp.broadcast_to(x4[h:h + 1], (8, n)) for h in range(MLA_HEADS)], axis=0)


def _softmax_step(s, pv_fn, m_ref, l_ref, acc_ref):
    m_prev = m_ref[...]
    m_new = jnp.maximum(m_prev, jnp.max(s, axis=-1, keepdims=True))
    p = jnp.exp(s - m_new)
    corr = jnp.exp(m_prev - m_new)
    l_ref[...] = corr * l_ref[...] + jnp.sum(p, axis=-1, keepdims=True)
    acc_ref[...] = corr * acc_ref[...] + pv_fn(p.astype(BF16))
    m_ref[...] = m_new


def _paged_scores(q, pages):
    return jnp.concatenate([jnp.dot(q, pages[p], preferred_element_type=F32) for p in range(pages.shape[0])],
                           axis=1)


def _paged_values(w, pages):
    out = None
    for p in range(pages.shape[0]):
        part = lax.dot_general(w[:, p * PAGE_SIZE:(p + 1) * PAGE_SIZE], pages[p], (((1,), (1,)), ((), ())),
                               preferred_element_type=F32)
        out = part if out is None else out + part
    return out


def _decode_loop(pt_ref, layer, srcs, bufs, sems, block_fn, n_blocks, init):
    b = pl.program_id(0)
    for c in _page_copies(pt_ref, layer, b, n_blocks - 1, 0, srcs, bufs, sems):
        c.start()

    def body(it, carry):
        blk = n_blocks - 1 - it
        slot = jnp.bitwise_and(it, 1)

        @pl.when(it + 1 < n_blocks)
        def _():
            for c in _page_copies(pt_ref, layer, b, blk - 1, 1 - slot, srcs, bufs, sems):
                c.start()

        for c in _page_copies(pt_ref, layer, b, blk, slot, srcs, bufs, sems):
            c.wait()
        return block_fn(slot, carry)

    return lax.fori_loop(0, n_blocks, body, init)


def _fox_decode_kernel(pt_ref, q_ref, kn_ref, vn_ref, bn_ref, gq_ref, k_hbm, v_hbm, lf_hbm, o_ref,
                       kbuf, vbuf, lfbuf, sems, m_ref, l_ref, acc_ref, *, n_blocks, scale, layer):
    m_ref[...] = jnp.full_like(m_ref, -jnp.inf)
    l_ref[...] = jnp.zeros_like(l_ref)
    acc_ref[...] = jnp.zeros_like(acc_ref)
    q = q_ref[0]
    gq = gq_ref[0]
    lane = lax.broadcasted_iota(jnp.int32, (FOX_HEADS, V7X_LANES), 1)
    def block(slot, suf):
        biases = [None] * DECODE_PAGES
        for p in reversed(range(DECODE_PAGES)):
            x = lfbuf[slot, p]
            y = x
            for d in (1, 2, 4, 8, 16, 32, 64):
                y = y + jnp.where(lane < V7X_LANES - d, pltpu.roll(y, V7X_LANES - d, 1), 0.0)
            biases[p] = (y - x) + suf
            suf = suf + jnp.broadcast_to(y[:, 0:1], (FOX_HEADS, V7X_LANES))
        bias = jnp.concatenate([_head_rows(bp) for bp in biases], axis=1)
        vb = vbuf[slot].astype(BF16)
        s = _paged_scores(q, kbuf[slot].astype(BF16)) * scale
        _softmax_step(s + bias + gq, lambda w: _paged_values(w, vb), m_ref, l_ref, acc_ref)
        return suf

    _decode_loop(pt_ref, layer, (k_hbm, v_hbm, lf_hbm), (kbuf, vbuf, lfbuf), (sems.at[0], sems.at[1], sems.at[2]),
                 block, n_blocks, jnp.zeros((FOX_HEADS, V7X_LANES), F32))
    s = lax.dot_general(q, kn_ref[0], (((1,), (1,)), ((), ())), preferred_element_type=F32) * scale
    _softmax_step(s + bn_ref[0], lambda w: jnp.dot(w, vn_ref[0], preferred_element_type=F32), m_ref, l_ref, acc_ref)
    o_ref[0] = acc_ref[...] / l_ref[...]


def _pages_rows_last(cache):
    nl, npool, rows = cache.shape[:3]
    return jnp.moveaxis(cache, 2, -1).reshape(nl, npool, -1, rows)


def fox_decode(q, k_new, v_new, lf_new, cache_k, cache_v, cache_lft, page_table, layer):
    bd, t, n_kv, n_grp, dk = q.shape
    nh = n_kv * n_grp
    assert nh * t == DECODE_ROWS and n_kv * dk == V7X_LANES and t == 8
    n_pages = page_table.shape[1]
    n_blocks = n_pages // DECODE_PAGES
    qh = jnp.transpose(q.reshape(bd, t, nh, dk), (0, 2, 1, 3))
    zero = jnp.zeros((bd, n_grp * t, dk), q.dtype)
    qpad = jnp.concatenate(
        [jnp.concatenate([zero] * hk + [qh[:, hk * n_grp:(hk + 1) * n_grp].reshape(bd, n_grp * t, dk)]
                         + [zero] * (n_kv - 1 - hk), axis=-1) for hk in range(n_kv)], axis=1).astype(BF16)
    pad_rows = ((0, 0), (0, PAGE_SIZE - t), (0, 0))
    kn = jnp.pad(k_new.reshape(bd, t, n_kv * dk), pad_rows).astype(BF16)
    vn = jnp.pad(v_new.reshape(bd, t, n_kv * dk), pad_rows).astype(BF16)
    g = jnp.transpose(jnp.cumsum(lf_new.reshape(bd, t, nh).astype(F32), axis=1), (0, 2, 1))
    causal = jnp.tril(jnp.ones((t, t), bool))
    bn = jnp.where(causal, g[:, :, :, None] - g[:, :, None, :], -jnp.inf).reshape(bd, nh * t, t)
    bn = jnp.pad(bn, ((0, 0), (0, 0), (0, PAGE_SIZE - t)), constant_values=-jnp.inf)
    gq = g.reshape(bd, nh * t, 1)
    row_spec = lambda shape: pl.BlockSpec((1,) + shape, lambda b, pt: (b, 0, 0))
    out = pl.pallas_call(
        functools.partial(_fox_decode_kernel, n_blocks=n_blocks, scale=dk ** -0.5, layer=layer),
        grid_spec=pltpu.PrefetchScalarGridSpec(
            num_scalar_prefetch=1,
            grid=(bd,),
            in_specs=[row_spec((DECODE_ROWS, V7X_LANES)), row_spec((PAGE_SIZE, V7X_LANES)),
                      row_spec((PAGE_SIZE, V7X_LANES)), row_spec((DECODE_ROWS, PAGE_SIZE)),
                      row_spec((DECODE_ROWS, 1)),
                      pl.BlockSpec(memory_space=pl.ANY), pl.BlockSpec(memory_space=pl.ANY),
                      pl.BlockSpec(memory_space=pl.ANY)],
            out_specs=row_spec((DECODE_ROWS, V7X_LANES)),
            scratch_shapes=[pltpu.VMEM((2, DECODE_PAGES, PAGE_SIZE, V7X_LANES), F32),
                            pltpu.VMEM((2, DECODE_PAGES, PAGE_SIZE, V7X_LANES), F32),
                            pltpu.VMEM((2, DECODE_PAGES, nh, PAGE_SIZE), F32),
                            pltpu.SemaphoreType.DMA((3, 2)),
                            pltpu.VMEM((DECODE_ROWS, 1), F32), pltpu.VMEM((DECODE_ROWS, 1), F32),
                            pltpu.VMEM((DECODE_ROWS, V7X_LANES), F32)]),
        out_shape=jax.ShapeDtypeStruct((bd, DECODE_ROWS, V7X_LANES), F32),
        compiler_params=pltpu.CompilerParams(dimension_semantics=("arbitrary",),
                                             vmem_limit_bytes=V7X_VMEM_LIMIT),
        name="fox_decode",
    )(page_table, qpad, kn, vn, bn, gq, _pages_rows_last(cache_k), _pages_rows_last(cache_v), cache_lft)
    o = out.reshape(bd, n_kv, n_grp, t, n_kv, dk)
    o = jnp.stack([o[:, hk, :, :, hk] for hk in range(n_kv)], axis=1)
    return jnp.transpose(o, (0, 3, 1, 2, 4))


def _mla_decode_kernel(pt_ref, qn_ref, qr_ref, latn_ref, krn_ref, bn_ref, wuk_ref, seg_ref, wuv_ref,
                       lat_hbm, kr_hbm, o_ref, latbuf, krbuf, sems, m_ref, l_ref, acc_ref, *, n_blocks, scale, layer):
    m_ref[...] = jnp.full_like(m_ref, -jnp.inf)
    l_ref[...] = jnp.zeros_like(l_ref)
    acc_ref[...] = jnp.zeros_like(acc_ref)
    qn = qn_ref[0]
    qr = qr_ref[0]

    def scores(latb, s_rope):
        kn = jnp.dot(latb, wuk_ref[...], preferred_element_type=F32)
        ssq = lax.dot_general(seg_ref[...], (kn * kn).astype(BF16), (((1,), (1,)), ((), ())),
                              preferred_element_type=F32)
        inv = lax.rsqrt(ssq * (1.0 / MLA_NOPE) + EPS)
        s_nope = lax.dot_general(qn, kn.astype(BF16), (((1,), (1,)), ((), ())), preferred_element_type=F32)
        return (s_nope * _head_rows(inv) + s_rope) * scale

    rows = DECODE_PAGES * PAGE_SIZE

    def block(slot, carry):
        latb = latbuf[slot].reshape(rows, MLA_KV_RANK).astype(BF16)
        s_rope = _paged_scores(qr, krbuf[slot].astype(BF16))
        _softmax_step(scores(latb, s_rope), lambda w: jnp.dot(w, latb, preferred_element_type=F32),
                      m_ref, l_ref, acc_ref)
        return carry

    _decode_loop(pt_ref, layer, (lat_hbm, kr_hbm), (latbuf, krbuf), (sems.at[0], sems.at[1]), block, n_blocks,
                 jnp.zeros((8, V7X_LANES), F32))
    s_rope = lax.dot_general(qr, krn_ref[0], (((1,), (1,)), ((), ())), preferred_element_type=F32)
    _softmax_step(scores(latn_ref[0], s_rope) + bn_ref[0],
                  lambda w: jnp.dot(w, latn_ref[0], preferred_element_type=F32), m_ref, l_ref, acc_ref)
    ctx = (acc_ref[...] / l_ref[...]).astype(BF16)
    o_ref[0] = jnp.dot(ctx, wuv_ref[...], preferred_element_type=F32)


def mla_decode(q, lat_new, kr_new, cache_lat, cache_kr, page_table, layer, w_uk, w_uv, g_kn):
    bd, t, nh, _, _ = q.shape
    assert nh * t == DECODE_ROWS and t == 8
    n_pages = page_table.shape[1]
    n_blocks = n_pages // DECODE_PAGES
    qh = jnp.transpose(q[:, :, :, 0, :], (0, 2, 1, 3))
    qnope = qh[..., :MLA_NOPE] * g_kn.astype(F32)
    zero = jnp.zeros((bd, t, MLA_NOPE), F32)
    qn = jnp.concatenate([jnp.concatenate([zero] * h + [qnope[:, h]] + [zero] * (nh - 1 - h), axis=-1)
                          for h in range(nh)], axis=1).astype(BF16)
    qr = qh[..., MLA_NOPE:].reshape(bd, nh * t, MLA_ROPE).astype(BF16)
    pad_rows = ((0, 0), (0, PAGE_SIZE - t), (0, 0))
    latn = jnp.pad(lat_new, pad_rows).astype(BF16)
    krn = jnp.pad(kr_new, pad_rows).astype(BF16)
    causal = jnp.tril(jnp.ones((t, t), bool))
    bn = jnp.broadcast_to(jnp.where(causal, 0.0, -jnp.inf)[None, None], (bd, nh, t, t)).reshape(bd, nh * t, t)
    bn = jnp.pad(bn.astype(F32), ((0, 0), (0, 0), (0, PAGE_SIZE - t)), constant_values=-jnp.inf)
    wuk = w_uk.reshape(MLA_KV_RANK, nh * MLA_NOPE).astype(BF16)
    wuv = w_uv.reshape(MLA_KV_RANK, nh * MLA_V).astype(BF16)
    seg = (jnp.arange(8)[:, None] == (jnp.arange(nh * MLA_NOPE) // MLA_NOPE)[None, :]).astype(BF16)
    row_spec = lambda shape: pl.BlockSpec((1,) + shape, lambda b, pt: (b, 0, 0))
    full = lambda shape: pl.BlockSpec(shape, lambda b, pt: (0, 0))
    out = pl.pallas_call(
        functools.partial(_mla_decode_kernel, n_blocks=n_blocks, scale=(MLA_NOPE + MLA_ROPE) ** -0.5,
                          layer=layer),
        grid_spec=pltpu.PrefetchScalarGridSpec(
            num_scalar_prefetch=1,
            grid=(bd,),
            in_specs=[row_spec((DECODE_ROWS, nh * MLA_NOPE)), row_spec((DECODE_ROWS, MLA_ROPE)),
                      row_spec((PAGE_SIZE, MLA_KV_RANK)), row_spec((PAGE_SIZE, MLA_ROPE)),
                      row_spec((DECODE_ROWS, PAGE_SIZE)),
                      full((MLA_KV_RANK, nh * MLA_NOPE)), full((8, nh * MLA_NOPE)), full((MLA_KV_RANK, nh * MLA_V)),
                      pl.BlockSpec(memory_space=pl.ANY), pl.BlockSpec(memory_space=pl.ANY)],
            out_specs=row_spec((DECODE_ROWS, nh * MLA_V)),
            scratch_shapes=[pltpu.VMEM((2, DECODE_PAGES, PAGE_SIZE, MLA_KV_RANK), F32),
                            pltpu.VMEM((2, DECODE_PAGES, MLA_ROPE, PAGE_SIZE), F32),
                            pltpu.SemaphoreType.DMA((2, 2)),
                            pltpu.VMEM((DECODE_ROWS, 1), F32), pltpu.VMEM((DECODE_ROWS, 1), F32),
                            pltpu.VMEM((DECODE_ROWS, MLA_KV_RANK), F32)]),
        out_shape=jax.ShapeDtypeStruct((bd, DECODE_ROWS, nh * MLA_V), F32),
        compiler_params=pltpu.CompilerParams(dimension_semantics=("arbitrary",),
                                             vmem_limit_bytes=V7X_VMEM_LIMIT),
        name="mla_decode",
    )(page_table, qn, qr, latn, krn, bn, wuk, seg, wuv, cache_lat, _pages_rows_last(cache_kr))
    o = out.reshape(bd, nh, t, nh, MLA_V)
    o = jnp.stack([o[:, h, :, h] for h in range(nh)], axis=1)
    return jnp.transpose(o, (0, 2, 1, 3))[:, :, :, None, :]


def _online_update(m, l, acc, s, v):
    m_new = jnp.maximum(m, s.max(-1))
    p = jnp.exp(s - m_new[..., None])
    corr = jnp.exp(m - m_new)
    acc = acc * corr[..., None] + jnp.einsum('bhgts,bshe->bhgte', p, v.astype(F32))
    return m_new, l * corr + p.sum(-1), acc


def decode_attention(q, k_new, v_new, lf_new, load_past, n_blocks):
    bd, t, n_kv, n_grp, dk = q.shape
    dv = v_new.shape[-1]
    scale = dk ** -0.5
    use_forget = lf_new is not None
    if use_forget:
        g_new = jnp.transpose(jnp.cumsum(lf_new, axis=1), (0, 2, 3, 1))
        g_q = g_new[..., :, None]

    def past_block(carry, j):
        m, l, acc = carry[:3]
        k, v, lf = load_past(j)
        s = jnp.einsum('bthgd,bshd->bhgts', q, k).astype(F32) * scale
        extra = ()
        if use_forget:
            suf = carry[3]
            tot = lf.sum(axis=1)
            s_past = suf[:, None] + tot[:, None] - jnp.cumsum(lf, axis=1)
            s = s + jnp.transpose(s_past, (0, 2, 3, 1))[:, :, :, None, :] + g_q
            extra = (suf + tot,)
        m, l, acc = _online_update(m, l, acc, s, v)
        return (m, l, acc) + extra, None

    init = (jnp.full((bd, n_kv, n_grp, t), -jnp.inf, F32), jnp.zeros((bd, n_kv, n_grp, t), F32),
            jnp.zeros((bd, n_kv, n_grp, t, dv), F32))
    if use_forget:
        init = init + (jnp.zeros((bd, n_kv, n_grp), F32),)
    carry, _ = lax.scan(past_block, init, jnp.arange(n_blocks), reverse=True)
    m, l, acc = carry[:3]
    s = jnp.einsum('bthgd,bshd->bhgts', q, k_new).astype(F32) * scale
    if use_forget:
        s = s + g_q - g_new[..., None, :]
    causal = jnp.tril(jnp.ones((t, t), bool))
    s = jnp.where(causal, s, -jnp.inf)
    m, l, acc = _online_update(m, l, acc, s, v_new)
    out = acc / l[..., None]
    return jnp.transpose(out, (0, 3, 1, 2, 4)).astype(v_new.dtype)


def mlstm_chunkwise(q, k, v, ig, lf, c0, n0, m0):
    bsz, seq, nh, dk = q.shape
    dv = v.shape[-1]
    ln = math.gcd(seq, MLSTM_CHUNK)
    nc = seq // ln
    q = q.astype(F32)
    k = k.astype(F32) * dk ** -0.5
    v = v.astype(F32)

    def chunks(a):
        return jnp.moveaxis(a.reshape((bsz, nc, ln) + a.shape[2:]), 1, 0)

    causal = jnp.tril(jnp.ones((ln, ln), bool))[None, :, :, None]

    def step(carry, inp):
        c, n, m = carry
        qc, kc, vc, ic, fc = inp
        b = jnp.cumsum(fc, axis=1)
        dmat = b[:, :, None, :] - b[:, None, :, :] + ic[:, None, :, :]
        dmat = jnp.where(causal, dmat, -jnp.inf)
        inter = b + m[:, None, :]
        mt = jnp.maximum(inter, dmat.max(axis=2))
        w = jnp.exp(dmat - mt[:, :, None, :])
        ei = jnp.exp(inter - mt)
        a = w * jnp.einsum('bthd,bshd->btsh', qc, kc)
        num = ei[..., None] * jnp.einsum('bhed,bthd->bthe', c, qc) + jnp.einsum('btsh,bshe->bthe', a, vc)
        den = ei * jnp.einsum('bhd,bthd->bth', n, qc) + a.sum(axis=2)
        h = num / jnp.maximum(jnp.abs(den), jnp.exp(-mt))[..., None]
        b_last = b[:, -1]
        wlog = b_last[:, None] - b + ic
        m_new = jnp.maximum(b_last + m, wlog.max(axis=1))
        ws = jnp.exp(wlog - m_new[:, None])
        decay = jnp.exp(b_last + m - m_new)
        c_new = decay[..., None, None] * c + jnp.einsum('bsh,bshe,bshd->bhed', ws, vc, kc)
        n_new = decay[..., None] * n + jnp.einsum('bsh,bshd->bhd', ws, kc)
        return (c_new, n_new, m_new), h

    init = (c0.astype(F32), n0.astype(F32), m0.astype(F32))
    (c1, n1, m1), hs = lax.scan(step, init, (chunks(q), chunks(k), chunks(v), chunks(ig), chunks(lf)))
    h = jnp.moveaxis(hs, 0, 1).reshape(bsz, seq, nh, dv)
    return h, c1, n1, m1


def _gelu(x):
    return 0.5 * x * (1.0 + lax.erf(x * (2.0 ** -0.5)))


S5_NCH = S5_GROUPS * S5_STATE
S5_SCAN_ROWS = 8


def _s5_scan_block(xr, xi, pr, pi, cr, ci, row):
    for dist in (1, 2, 4):
        ar = pr[dist - 1:dist]
        ai = pi[dist - 1:dist]
        sr = jnp.where(row >= dist, pltpu.roll(xr, dist, 0), 0.0)
        si = jnp.where(row >= dist, pltpu.roll(xi, dist, 0), 0.0)
        xr, xi = xr + ar * sr - ai * si, xi + ar * si + ai * sr
    return xr + pr * cr - pi * ci, xi + pr * ci + pi * cr


def _s5_readout(u, xr_ref, xi_ref, cre_ref, cim_ref, d_ref, wglu_ref, bglu_ref):
    y = (jnp.dot(xr_ref[...].astype(BF16), cre_ref[...], preferred_element_type=F32)
         - jnp.dot(xi_ref[...].astype(BF16), cim_ref[...], preferred_element_type=F32)
         + u * d_ref[...])
    y = _gelu(y)
    gate = jnp.dot(y.astype(BF16), wglu_ref[...], preferred_element_type=F32) + bglu_ref[...]
    return y * jax.nn.sigmoid(gate)


def _s5_project(u, bre_ref, bim_ref, xr_ref, xi_ref):
    xr_ref[...] = jnp.dot(u, bre_ref[...], preferred_element_type=F32, precision=lax.Precision.HIGHEST)
    xi_ref[...] = jnp.dot(u, bim_ref[...], preferred_element_type=F32, precision=lax.Precision.HIGHEST)


def _s5_seq_kernel(u_ref, bre_ref, bim_ref, cre_ref, cim_ref, d_ref, wglu_ref, bglu_ref, p_ref, x0_ref,
                   o_ref, st_ref, xr_ref, xi_ref, car_ref):
    ts = u_ref.shape[1]
    u = u_ref[0]
    _s5_project(u, bre_ref, bim_ref, xr_ref, xi_ref)

    @pl.when(pl.program_id(1) == 0)
    def _():
        car_ref[...] = x0_ref[0]

    pr = p_ref[0]
    pi = p_ref[1]
    row = lax.broadcasted_iota(jnp.int32, (S5_SCAN_ROWS, S5_NCH), 0)

    def blk(j, carry):
        cr, ci = carry
        r0 = pl.multiple_of(j * S5_SCAN_ROWS, S5_SCAN_ROWS)
        xr, xi = _s5_scan_block(xr_ref[pl.ds(r0, S5_SCAN_ROWS), :], xi_ref[pl.ds(r0, S5_SCAN_ROWS), :],
                                pr, pi, cr, ci, row)
        xr_ref[pl.ds(r0, S5_SCAN_ROWS), :] = xr
        xi_ref[pl.ds(r0, S5_SCAN_ROWS), :] = xi
        return xr[S5_SCAN_ROWS - 1:], xi[S5_SCAN_ROWS - 1:]

    cr, ci = lax.fori_loop(0, ts // S5_SCAN_ROWS, blk, (car_ref[0:1], car_ref[1:2]))
    car_ref[0:1] = cr
    car_ref[1:2] = ci
    st_ref[0] = car_ref[...]
    o_ref[0] = _s5_readout(u, xr_ref, xi_ref, cre_ref, cim_ref, d_ref, wglu_ref, bglu_ref)


def _s5_short_kernel(u_ref, bre_ref, bim_ref, cre_ref, cim_ref, d_ref, wglu_ref, bglu_ref, p_ref, x0_ref,
                     o_ref, st_ref, xr_ref, xi_ref):
    nseq = x0_ref.shape[0]
    u = u_ref[...]
    _s5_project(u, bre_ref, bim_ref, xr_ref, xi_ref)
    pr = p_ref[0]
    pi = p_ref[1]
    row = lax.broadcasted_iota(jnp.int32, (S5_SCAN_ROWS, S5_NCH), 0)

    def blk(j, carry):
        r0 = pl.multiple_of(j * S5_SCAN_ROWS, S5_SCAN_ROWS)
        x0 = x0_ref[j]
        xr, xi = _s5_scan_block(xr_ref[pl.ds(r0, S5_SCAN_ROWS), :], xi_ref[pl.ds(r0, S5_SCAN_ROWS), :],
                                pr, pi, x0[0:1], x0[1:2], row)
        xr_ref[pl.ds(r0, S5_SCAN_ROWS), :] = xr
        xi_ref[pl.ds(r0, S5_SCAN_ROWS), :] = xi
        st_ref[j, 0:1, :] = xr[S5_SCAN_ROWS - 1:]
        st_ref[j, 1:2, :] = xi[S5_SCAN_ROWS - 1:]
        return carry

    lax.fori_loop(0, nseq, blk, 0)
    o_ref[...] = _s5_readout(u, xr_ref, xi_ref, cre_ref, cim_ref, d_ref, wglu_ref, bglu_ref)


def _s5_tables(a_re, a_im, log_step, b_re, b_im, c_re, c_im):
    step = jnp.exp(log_step.astype(F32))[:, None]
    mag = jnp.exp(a_re * step)
    abr = mag * jnp.cos(a_im * step)
    abi = mag * jnp.sin(a_im * step)
    den = a_re * a_re + a_im * a_im
    cr = ((abr - 1.0) * a_re + abi * a_im) / den
    ci = (abi * a_re - (abr - 1.0) * a_im) / den
    bbr = cr[..., None] * b_re - ci[..., None] * b_im
    bbi = cr[..., None] * b_im + ci[..., None] * b_re
    eye = jnp.eye(S5_GROUPS, dtype=F32)
    bre = jnp.einsum('gpi,gh->gihp', bbr, eye).reshape(S5_CH, S5_NCH)
    bim = jnp.einsum('gpi,gh->gihp', bbi, eye).reshape(S5_CH, S5_NCH)
    cre = jnp.einsum('gop,gh->gpho', c_re, eye).reshape(S5_NCH, S5_CH).astype(BF16)
    cim = jnp.einsum('gop,gh->gpho', c_im, eye).reshape(S5_NCH, S5_CH).astype(BF16)
    ar = abr.reshape(1, S5_NCH)
    ai = abi.reshape(1, S5_NCH)
    pws = [(ar, ai)]
    for _ in range(S5_SCAN_ROWS - 1):
        qr, qi = pws[-1]
        pws.append((qr * ar - qi * ai, qr * ai + qi * ar))
    powers = jnp.stack([jnp.concatenate([p[0] for p in pws], axis=0),
                        jnp.concatenate([p[1] for p in pws], axis=0)], axis=0)
    return bre, bim, cre, cim, powers


def s5_layer(u, a_re, a_im, log_step, b_re, b_im, c_re, c_im, d, w_glu, b_glu, x0_re, x0_im):
    bsz, seq, _ = u.shape
    bre, bim, cre, cim, powers = _s5_tables(a_re, a_im, log_step, b_re, b_im, c_re, c_im)
    x0 = jnp.stack([x0_re.reshape(bsz, S5_NCH), x0_im.reshape(bsz, S5_NCH)], axis=1)
    d2 = d.astype(F32).reshape(1, S5_CH)
    bg = b_glu.astype(F32).reshape(1, S5_CH)
    wg = w_glu.astype(BF16)
    const = lambda *_: (0, 0)
    weight_specs = [pl.BlockSpec((S5_CH, S5_NCH), const), pl.BlockSpec((S5_CH, S5_NCH), const),
                    pl.BlockSpec((S5_NCH, S5_CH), const), pl.BlockSpec((S5_NCH, S5_CH), const),
                    pl.BlockSpec((1, S5_CH), const), pl.BlockSpec((S5_CH, S5_CH), const),
                    pl.BlockSpec((1, S5_CH), const),
                    pl.BlockSpec((2, S5_SCAN_ROWS, S5_NCH), lambda *_: (0, 0, 0))]
    weights = (bre, bim, cre, cim, d2, wg, bg, powers)
    if seq == S5_SCAN_ROWS:
        nseq = math.gcd(bsz, 64)
        rows = nseq * seq
        out, st = pl.pallas_call(
            _s5_short_kernel,
            grid=(bsz // nseq,),
            in_specs=[pl.BlockSpec((rows, S5_CH), lambda i: (i, 0))] + weight_specs
                     + [pl.BlockSpec((nseq, 2, S5_NCH), lambda i: (i, 0, 0))],
            out_specs=[pl.BlockSpec((rows, S5_CH), lambda i: (i, 0)),
                       pl.BlockSpec((nseq, 2, S5_NCH), lambda i: (i, 0, 0))],
            out_shape=[jax.ShapeDtypeStruct((bsz * seq, S5_CH), F32),
                       jax.ShapeDtypeStruct((bsz, 2, S5_NCH), F32)],
            scratch_shapes=[pltpu.VMEM((rows, S5_NCH), F32), pltpu.VMEM((rows, S5_NCH), F32)],
            compiler_params=pltpu.CompilerParams(dimension_semantics=("arbitrary",),
                                                 vmem_limit_bytes=V7X_VMEM_LIMIT),
            name="s5_short",
        )(u.reshape(bsz * seq, S5_CH), *weights, x0)
        out = out.reshape(bsz, seq, S5_CH)
    else:
        ts = math.gcd(seq, 512)
        out, st = pl.pallas_call(
            _s5_seq_kernel,
            grid=(bsz, seq // ts),
            in_specs=[pl.BlockSpec((1, ts, S5_CH), lambda b, s: (b, s, 0))] + weight_specs
                     + [pl.BlockSpec((1, 2, S5_NCH), lambda b, s: (b, 0, 0))],
            out_specs=[pl.BlockSpec((1, ts, S5_CH), lambda b, s: (b, s, 0)),
                       pl.BlockSpec((1, 2, S5_NCH), lambda b, s: (b, 0, 0))],
            out_shape=[jax.ShapeDtypeStruct((bsz, seq, S5_CH), F32),
                       jax.ShapeDtypeStruct((bsz, 2, S5_NCH), F32)],
            scratch_shapes=[pltpu.VMEM((ts, S5_NCH), F32), pltpu.VMEM((ts, S5_NCH), F32),
                            pltpu.VMEM((2, S5_NCH), F32)],
            compiler_params=pltpu.CompilerParams(dimension_semantics=("arbitrary", "arbitrary"),
                                                 vmem_limit_bytes=V7X_VMEM_LIMIT),
            name="s5_seq",
        )(u, *weights, x0)
    s_re = st[:, 0].reshape(bsz, S5_GROUPS, S5_STATE)
    s_im = st[:, 1].reshape(bsz, S5_GROUPS, S5_STATE)
    return out, s_re, s_im


def mla_keys(lat, krope, w_uk, w_uv, g_kn):
    k_nope = rmsnorm(jnp.einsum('bsr,rhd->bshd', lat, w_uk), g_kn)
    v = jnp.einsum('bsr,rhd->bshd', lat, w_uv)
    kr = jnp.broadcast_to(krope[:, :, None, :], k_nope.shape[:3] + (MLA_ROPE,)).astype(k_nope.dtype)
    return jnp.concatenate([k_nope, kr], axis=-1), v


PEER_HALF = PEER_KEY_DIM // 2
PEER_QW = PEER_HEADS * PEER_KEY_DIM
PEER_ROUTE_TOKENS = 256
PEER_TOKENS = 512
PEER_KEY_ROWS = 8
PEER_EXPERT_BLOCK = PEER_KEY_ROWS * PEER_NKEYS
PEER_ROW_BLOCK = 128


def _extract_top(xs, rows_f):
    slot = lax.broadcasted_iota(jnp.int32, (PEER_TOPK, V7X_LANES), 0)
    nrows = float(xs[0].shape[0])

    def body(r, carry):
        out = []
        for x, sv in carry:
            m = jnp.max(x, axis=0, keepdims=True)
            first = jnp.min(jnp.where(x == m, rows_f, nrows), axis=0, keepdims=True)
            out.append((jnp.where(rows_f == first, -jnp.inf, x), jnp.where(slot == r, m, sv)))
        return tuple(out)

    init = tuple((x, jnp.zeros((PEER_TOPK, V7X_LANES), F32)) for x in xs)
    return [sv for _, sv in lax.fori_loop(0, PEER_TOPK, body, init)]


PEER_PAIR_PIECES = tuple((b, PEER_TOPK // (b + 1)) for b in range(1, 8))
PEER_PAIR_ROWS = PEER_TOPK + 8 * len(PEER_PAIR_PIECES) + 8


def _pair_candidates(sv0, sv1):
    row8 = lax.broadcasted_iota(jnp.int32, (8, V7X_LANES), 0)
    pieces = [sv0 + sv1[0:1]]
    for b, n_valid in PEER_PAIR_PIECES:
        pieces.append(jnp.where(row8 < n_valid, sv0[0:8] + sv1[b:b + 1], -jnp.inf))
    pieces.append(sv0[0:1] + sv1[8:16])
    return jnp.concatenate(pieces, axis=0)


def _peer_route_kernel(h_ref, g_ref, wqt_ref, sk_ref, xb_ref, s0_ref, scl_ref, s1_ref, e1_ref, tau_ref,
                       st_ref):
    tr = h_ref.shape[0]
    hf = h_ref[...]
    c = hf * lax.rsqrt(jnp.mean(hf * hf, axis=-1, keepdims=True) + EPS) * g_ref[...]
    cb = c.astype(BF16)
    xb_ref[...] = cb
    qt = lax.dot_general(wqt_ref[...], cb, (((1,), (1,)), ((), ())), preferred_element_type=F32)
    for hc in range(2 * PEER_HEADS):
        st_ref[hc] = jnp.dot(sk_ref[hc], qt[hc * PEER_HALF:(hc + 1) * PEER_HALF].astype(BF16),
                             preferred_element_type=F32)

    rows_key = lax.broadcasted_iota(jnp.int32, (PEER_NKEYS, V7X_LANES), 0).astype(F32)
    rows_pair = lax.broadcasted_iota(jnp.int32, (PEER_PAIR_ROWS, V7X_LANES), 0).astype(F32)
    lane_groups = [slice(lg * V7X_LANES, (lg + 1) * V7X_LANES) for lg in range(tr // V7X_LANES)]

    def head(h, carry):
        tops = []
        for lanes in lane_groups:
            s0 = st_ref[2 * h, :, lanes]
            s1 = st_ref[2 * h + 1, :, lanes]
            tops.append(_extract_top([s0, s1], rows_key))
        bests = _extract_top([_pair_candidates(sv0, sv1) for sv0, sv1 in tops], rows_pair)
        for lanes, (sv0, sv1), best in zip(lane_groups, tops, bests):
            s0 = st_ref[2 * h, :, lanes]
            s1 = st_ref[2 * h + 1, :, lanes]
            z = jnp.sum(jnp.exp(best - best[0:1]), axis=0, keepdims=True)
            s0_ref[h, :, lanes] = s0
            scl_ref[h, :, lanes] = jnp.exp(s0 - sv0[0:1]) / z
            s1_ref[h, :, lanes] = s1
            e1_ref[h, :, lanes] = jnp.exp(s1 - sv1[0:1])
            tau_ref[h, :, lanes] = best[PEER_TOPK - 1:PEER_TOPK]
        return carry

    lax.fori_loop(0, PEER_HEADS, head, 0)


def _peer_expert_kernel(x_ref, u_ref, vt_ref, s0_ref, scl_ref, s1_ref, e1_ref, tau_ref, o_ref,
                        acc_ref, a_ref, s_ref):
    ei = pl.program_id(1)
    tt = x_ref.shape[0]
    key_rows = u_ref.shape[0] // PEER_NKEYS

    @pl.when(ei == 0)
    def _():
        acc_ref[...] = jnp.zeros_like(acc_ref)

    s_ref[...] = lax.dot_general(u_ref[...], x_ref[...], (((1,), (1,)), ((), ())), preferred_element_type=F32)

    assert key_rows == PEER_KEY_ROWS
    first_keys = pl.ds(pl.multiple_of(ei * PEER_KEY_ROWS, PEER_KEY_ROWS), PEER_KEY_ROWS)
    for lg in range(tt // V7X_LANES):
        lanes = slice(lg * V7X_LANES, (lg + 1) * V7X_LANES)
        s0rows = [s0_ref[h, first_keys, lanes] for h in range(PEER_HEADS)]
        sclrows = [scl_ref[h, first_keys, lanes] for h in range(PEER_HEADS)]
        taurow = [tau_ref[h, :, lanes] for h in range(PEER_HEADS)]
        for r in range(key_rows):
            s0row = [v[r:r + 1] for v in s0rows]
            sclrow = [v[r:r + 1] for v in sclrows]
            for jb in range(PEER_NKEYS // PEER_ROW_BLOCK):
                rs = slice(jb * PEER_ROW_BLOCK, (jb + 1) * PEER_ROW_BLOCK)
                es = slice(r * PEER_NKEYS + jb * PEER_ROW_BLOCK, r * PEER_NKEYS + (jb + 1) * PEER_ROW_BLOCK)
                gate = jnp.zeros((PEER_ROW_BLOCK, V7X_LANES), F32)
                for h in range(PEER_HEADS):
                    pair = s1_ref[h, rs, lanes] + s0row[h]
                    gate = gate + jnp.where(pair >= taurow[h], e1_ref[h, rs, lanes], 0.0) * sclrow[h]
                a_ref[es, lanes] = (gate * _gelu(s_ref[es, lanes])).astype(BF16)

    acc_ref[...] += jnp.dot(vt_ref[...], a_ref[...], preferred_element_type=F32)

    @pl.when(ei == pl.num_programs(1) - 1)
    def _():
        o_ref[...] = acc_ref[...].T


def _peer_tables_kernel(u_ref, v_ref, ub_ref, vt_ref):
    ub_ref[...] = u_ref[...].astype(BF16)
    vt_ref[...] = v_ref[...].T.astype(BF16)


def peer_tables(u_tab, v_tab):
    nexp, dm = u_tab.shape
    te = math.gcd(nexp, 512)
    return pl.pallas_call(
        _peer_tables_kernel,
        grid=(nexp // te,),
        in_specs=[pl.BlockSpec((te, dm), lambda i: (i, 0)), pl.BlockSpec((te, dm), lambda i: (i, 0))],
        out_specs=[pl.BlockSpec((te, dm), lambda i: (i, 0)), pl.BlockSpec((dm, te), lambda i: (0, i))],
        out_shape=[jax.ShapeDtypeStruct((nexp, dm), BF16), jax.ShapeDtypeStruct((dm, nexp), BF16)],
        compiler_params=pltpu.CompilerParams(dimension_semantics=("arbitrary",),
                                             vmem_limit_bytes=V7X_VMEM_LIMIT),
        name="peer_tables",
    )(u_tab, v_tab)


def peer_ffn(hres, g_ffn, w_q, subkeys, ub, vt):
    bsz, seq, dm = hres.shape
    ntok = bsz * seq
    nexp = ub.shape[0]
    tr = math.gcd(ntok, PEER_ROUTE_TOKENS)
    tt = math.gcd(ntok, PEER_TOKENS)
    wqt = w_q.T.astype(BF16)
    sk = subkeys.reshape(2 * PEER_HEADS, PEER_NKEYS, PEER_HALF).astype(BF16)
    key_shape = (PEER_HEADS, PEER_NKEYS, ntok)
    key_block = lambda t: pl.BlockSpec((PEER_HEADS, PEER_NKEYS, t), lambda i, *_: (0, 0, i))
    tau_block = lambda t: pl.BlockSpec((PEER_HEADS, 1, t), lambda i, *_: (0, 0, i))
    xb, s0t, sclt, s1t, e1t, taut = pl.pallas_call(
        _peer_route_kernel,
        grid=(ntok // tr,),
        in_specs=[pl.BlockSpec((tr, dm), lambda i: (i, 0)),
                  pl.BlockSpec((1, dm), lambda i: (0, 0)),
                  pl.BlockSpec((PEER_QW, dm), lambda i: (0, 0)),
                  pl.BlockSpec((2 * PEER_HEADS, PEER_NKEYS, PEER_HALF), lambda i: (0, 0, 0))],
        out_specs=[pl.BlockSpec((tr, dm), lambda i: (i, 0)),
                   key_block(tr), key_block(tr), key_block(tr), key_block(tr), tau_block(tr)],
        out_shape=[jax.ShapeDtypeStruct((ntok, dm), BF16)] + [jax.ShapeDtypeStruct(key_shape, F32)] * 4
                  + [jax.ShapeDtypeStruct((PEER_HEADS, 1, ntok), F32)],
        scratch_shapes=[pltpu.VMEM((2 * PEER_HEADS, PEER_NKEYS, tr), F32)],
        compiler_params=pltpu.CompilerParams(dimension_semantics=("arbitrary",),
                                             vmem_limit_bytes=V7X_VMEM_LIMIT),
        name="peer_route",
    )(hres.reshape(ntok, dm), g_ffn.astype(F32).reshape(1, dm), wqt, sk)

    eb = PEER_EXPERT_BLOCK
    out = pl.pallas_call(
        _peer_expert_kernel,
        grid=(ntok // tt, nexp // eb),
        in_specs=[pl.BlockSpec((tt, dm), lambda t, e: (t, 0)),
                  pl.BlockSpec((eb, dm), lambda t, e: (e, 0)),
                  pl.BlockSpec((dm, eb), lambda t, e: (0, e)),
                  key_block(tt), key_block(tt), key_block(tt), key_block(tt), tau_block(tt)],
        out_specs=pl.BlockSpec((tt, dm), lambda t, e: (t, 0)),
        out_shape=jax.ShapeDtypeStruct((ntok, dm), F32),
        scratch_shapes=[pltpu.VMEM((dm, tt), F32), pltpu.VMEM((eb, tt), BF16), pltpu.VMEM((eb, tt), F32)],
        compiler_params=pltpu.CompilerParams(dimension_semantics=("arbitrary", "arbitrary"),
                                             vmem_limit_bytes=V7X_VMEM_LIMIT),
        name="peer_experts",
    )(xb, ub, vt, s0t, sclt, s1t, e1t, taut)
    return out.reshape(bsz, seq, dm)


def mixing_sublayer(a, pos, lp, past):
    bsz, seq, _ = a.shape
    z = matmul3(a, lp['w_in'])
    points = [int(t) for t in np.cumsum(SPLIT_SIZES)[:-1]]
    (fq, fk, fv, ff, mq, mk, mv, mi, mf, mo, su, cq, ckv, kr) = jnp.split(z, points, axis=-1)

    q_f = rmsnorm(fq.reshape(bsz, seq, FOX_KV_HEADS, FOX_GROUP, HEAD_DIM), lp['g_fox_q'])
    k_f = rmsnorm(fk.reshape(bsz, seq, FOX_KV_HEADS, HEAD_DIM), lp['g_fox_k'])
    v_f = fv.reshape(bsz, seq, FOX_KV_HEADS, HEAD_DIM)
    lf_f = jax.nn.log_sigmoid(ff.astype(F32) + lp['b_fox_f'].astype(F32))
    lf_g = lf_f.reshape(bsz, seq, FOX_KV_HEADS, FOX_GROUP)

    c_q = rmsnorm(cq, lp['g_mla_cq'])
    q_m = jnp.einsum('bsr,rhd->bshd', c_q, lp['w_mla_uq'])
    q_m = jnp.concatenate([rmsnorm(q_m[..., :MLA_NOPE], lp['g_mla_qn']),
                           rope(rmsnorm(q_m[..., MLA_NOPE:], lp['g_mla_qr']), pos)], axis=-1)[:, :, :, None, :]
    lat = rmsnorm(ckv, lp['g_mla_ckv'])
    krope = rope(rmsnorm(kr, lp['g_mla_kr'])[:, :, None, :], pos)[:, :, 0, :]
    k_m, v_m = mla_keys(lat, krope, lp['w_mla_uk'], lp['w_mla_uv'], lp['g_mla_kn'])

    if past is None:
        o_f = prompt_attention(q_f, k_f, v_f, jnp.cumsum(lf_g, axis=1))
        o_m = prompt_attention(q_m, k_m, v_m, None)
        c0 = jnp.zeros((bsz, MLSTM_HEADS, HEAD_DIM, HEAD_DIM), F32)
        n0 = jnp.zeros((bsz, MLSTM_HEADS, HEAD_DIM), F32)
        m0 = jnp.zeros((bsz, MLSTM_HEADS), F32)
        x0_re = jnp.zeros((bsz, S5_GROUPS, S5_STATE), F32)
        x0_im = jnp.zeros((bsz, S5_GROUPS, S5_STATE), F32)
    else:
        o_f = fox_decode(q_f, k_f, v_f, lf_g, past['fox_k'], past['fox_v'], past['fox_lft'], past['page_table'],
                         past['layer'])
        o_m = mla_decode(q_m, lat, krope, past['mla_lat'], past['mla_kr'], past['page_table'], past['layer'],
                         lp['w_mla_uk'], lp['w_mla_uv'], lp['g_mla_kn'])
        c0, n0, m0, x0_re, x0_im = past['rec']

    ig = mi.astype(F32) + lp['b_mlstm_i'].astype(F32)
    lf_l = jax.nn.log_sigmoid(mf.astype(F32) + lp['b_mlstm_f'].astype(F32))
    h_l, c1, n1, m1 = mlstm_chunkwise(mq.reshape(bsz, seq, MLSTM_HEADS, HEAD_DIM),
                                      mk.reshape(bsz, seq, MLSTM_HEADS, HEAD_DIM),
                                      mv.reshape(bsz, seq, MLSTM_HEADS, HEAD_DIM), ig, lf_l, c0, n0, m0)
    o_l = rmsnorm(h_l, lp['g_mlstm_h']).astype(a.dtype).reshape(bsz, seq, MLSTM_W) * jax.nn.sigmoid(mo)

    o_s, s_re, s_im = s5_layer(su, lp['s5_a_re'], lp['s5_a_im'], lp['s5_log_step'], lp['s5_b_re'], lp['s5_b_im'],
                               lp['s5_c_re'], lp['s5_c_im'], lp['s5_d'], lp['w_glu'], lp['b_glu'], x0_re, x0_im)

    mix = jnp.concatenate([o_f.reshape(bsz, seq, -1), o_l, o_s, o_m.reshape(bsz, seq, -1)], axis=-1)
    out = matmul3(mix, lp['w_out'])
    return out, (k_f, v_f, lf_f, lat, krope, c1, n1, m1, s_re, s_im)


def run_trunk(x, p, pos, layer_params, pasts):
    h = x
    states = []
    for i in range(DEPTH):
        lp = layer_params[i]
        mix, st = mixing_sublayer(rmsnorm(h, lp['g_mix']), pos, lp, pasts[i])
        h = h + mix
        h = h + peer_ffn(h, lp['g_ffn'], lp['w_peer_q'], lp['peer_subkeys'], lp['peer_ub'], lp['peer_vt'])
        gate = jax.nn.sigmoid(matmul3(rmsnorm(h, lp['g_ple']), lp['w_ple_gate']))
        h = h + matmul3(p[i], lp['w_ple']) * gate
        states.append(st)
    stacked = [jnp.stack([st[j] for st in states], axis=0) for j in range(len(states[0]))]
    return h, stacked


def kernel(x_prompt, x_sample, cache_fox_k, cache_fox_v, cache_fox_logf, cache_mla_latent, cache_mla_krope,
           state_mlstm_C, state_mlstm_n, state_mlstm_m, state_s5_re, state_s5_im, page_table, p_prompt, p_sample,
           g_mix, w_in, g_fox_q, g_fox_k, b_fox_f, b_mlstm_i, b_mlstm_f, g_mlstm_h,
           s5_a_re, s5_a_im, s5_log_step, s5_b_re, s5_b_im, s5_c_re, s5_c_im, s5_d, w_glu, b_glu,
           g_mla_cq, w_mla_uq, g_mla_qn, g_mla_qr, g_mla_ckv, g_mla_kr, w_mla_uk, w_mla_uv, g_mla_kn,
           w_out, g_ffn, w_peer_q, peer_subkeys, peer_u, peer_v, g_ple, w_ple_gate, w_ple):
    names = ('g_mix', 'w_in', 'g_fox_q', 'g_fox_k', 'b_fox_f', 'b_mlstm_i', 'b_mlstm_f', 'g_mlstm_h',
             's5_a_re', 's5_a_im', 's5_log_step', 's5_b_re', 's5_b_im', 's5_c_re', 's5_c_im', 's5_d', 'w_glu',
             'b_glu', 'g_mla_cq', 'w_mla_uq', 'g_mla_qn', 'g_mla_qr', 'g_mla_ckv', 'g_mla_kr', 'w_mla_uk',
             'w_mla_uv', 'g_mla_kn', 'w_out', 'g_ffn', 'w_peer_q', 'peer_subkeys', 'peer_u', 'peer_v', 'g_ple',
             'w_ple_gate', 'w_ple')
    vals = (g_mix, w_in, g_fox_q, g_fox_k, b_fox_f, b_mlstm_i, b_mlstm_f, g_mlstm_h,
            s5_a_re, s5_a_im, s5_log_step, s5_b_re, s5_b_im, s5_c_re, s5_c_im, s5_d, w_glu,
            b_glu, g_mla_cq, w_mla_uq, g_mla_qn, g_mla_qr, g_mla_ckv, g_mla_kr, w_mla_uk,
            w_mla_uv, g_mla_kn, w_out, g_ffn, w_peer_q, peer_subkeys, peer_u, peer_v, g_ple,
            w_ple_gate, w_ple)
    layer_params = [{n: v[i] for n, v in zip(names, vals)} for i in range(DEPTH)]
    for lp in layer_params:
        lp['peer_ub'], lp['peer_vt'] = peer_tables(lp['peer_u'], lp['peer_v'])

    past_len = page_table.shape[1] * PAGE_SIZE
    cache_lft = jnp.swapaxes(cache_fox_logf, 2, 3)

    def make_past(i):
        return {'fox_k': cache_fox_k, 'fox_v': cache_fox_v, 'fox_lft': cache_lft, 'layer': i,
                'mla_lat': cache_mla_latent, 'mla_kr': cache_mla_krope, 'page_table': page_table,
                'rec': (state_mlstm_C[i], state_mlstm_n[i], state_mlstm_m[i], state_s5_re[i], state_s5_im[i])}

    pasts = [make_past(i) for i in range(DEPTH)]
    pos_s = past_len + jnp.arange(x_sample.shape[1], dtype=jnp.int32)
    y_sample, st_s = run_trunk(x_sample, p_sample, pos_s, layer_params, pasts)

    pos_p = jnp.arange(x_prompt.shape[1], dtype=jnp.int32)
    y_prompt, st_p = run_trunk(x_prompt, p_prompt, pos_p, layer_params, [None] * DEPTH)

    (pk, pv, plf, plat, pkr, pc, pn, pm, pre, pim) = st_p
    (sk, sv, slf, slat, skr, sc, sn, sm, sre, sim) = st_s
    return (y_prompt, y_sample, pk, pv, plf, plat, pkr, pc, pn, pm, pre, pim,
            sk, sv, slf, slat, skr, sc, sn, sm, sre, sim)
```

```python
import functools
import math

import jax
import jax.numpy as jnp
import numpy as np
from jax import lax
from jax.experimental import pallas as pl
from jax.experimental.pallas import tpu as pltpu

D_MODEL = 1024
DEPTH = 2
PAGE_SIZE = 128
HEAD_DIM = 64
GROUP_WIDTH = D_MODEL // 4
FOX_HEADS = GROUP_WIDTH // HEAD_DIM
FOX_KV_HEADS = FOX_HEADS // 2
FOX_GROUP = FOX_HEADS // FOX_KV_HEADS
MLSTM_HEADS = GROUP_WIDTH // HEAD_DIM
MLSTM_W = MLSTM_HEADS * HEAD_DIM
MLSTM_CHUNK = 64
S5_CH = GROUP_WIDTH
S5_GROUP_SIZE = 16
S5_GROUPS = S5_CH // S5_GROUP_SIZE
S5_STATE = 64
MLA_HEADS = GROUP_WIDTH // HEAD_DIM
MLA_NOPE = HEAD_DIM
MLA_ROPE = HEAD_DIM // 2
MLA_V = HEAD_DIM
MLA_Q_RANK = D_MODEL // 8
MLA_KV_RANK = D_MODEL // 8
ROPE_THETA = 10000.0
PEER_HEADS = 8
PEER_NKEYS = 128
PEER_KEY_DIM = 256
PEER_TOPK = 16
PEER_TOKEN_BLOCK = 256
Q_BLOCK = 128
MAX_KEY_BLOCK_PAGES = 8
EPS = 1e-6
F32 = jnp.float32
BF16 = jnp.bfloat16

SPLIT_SIZES = (FOX_HEADS * HEAD_DIM, FOX_KV_HEADS * HEAD_DIM, FOX_KV_HEADS * HEAD_DIM, FOX_HEADS,
               MLSTM_W, MLSTM_W, MLSTM_W, MLSTM_HEADS, MLSTM_HEADS, MLSTM_W,
               S5_CH,
               MLA_Q_RANK, MLA_KV_RANK, MLA_ROPE)

V7X_LANES = 128
V7X_VMEM_LIMIT = 48 * 1024 * 1024


def _mm_kernel(x_ref, w_ref, o_ref):
    o_ref[...] = jnp.dot(x_ref[...].astype(BF16), w_ref[...], preferred_element_type=F32)


def _norm_mm_kernel(x_ref, g_ref, w_ref, o_ref):
    x = x_ref[...]
    a = x * lax.rsqrt(jnp.mean(x * x, axis=-1, keepdims=True) + EPS) * g_ref[...]
    o_ref[...] = jnp.dot(a.astype(BF16), w_ref[...], preferred_element_type=F32)


def _row_tile(n):
    for t in (512, 256, 128, 64, 32, 16, 8):
        if n % t == 0:
            return t
    raise ValueError(f"row count {n} is not a multiple of 8")


def matmul(x, w, gain=None):
    n, k = x.shape
    m = w.shape[1]
    m_pad = -(-m // V7X_LANES) * V7X_LANES
    wb = w.astype(BF16)
    if m_pad != m:
        wb = jnp.pad(wb, ((0, 0), (0, m_pad - m)))
    tm = _row_tile(n)
    x_spec = pl.BlockSpec((tm, k), lambda i: (i, 0))
    w_spec = pl.BlockSpec((k, m_pad), lambda i: (0, 0))
    if gain is None:
        body, in_specs, operands = _mm_kernel, [x_spec, w_spec], (x, wb)
    else:
        body, in_specs = _norm_mm_kernel, [x_spec, pl.BlockSpec((1, k), lambda i: (0, 0)), w_spec]
        operands = (x, gain.astype(F32).reshape(1, k), wb)
    out = pl.pallas_call(
        body,
        grid=(n // tm,),
        in_specs=in_specs,
        out_specs=pl.BlockSpec((tm, m_pad), lambda i: (i, 0)),
        out_shape=jax.ShapeDtypeStruct((n, m_pad), F32),
        compiler_params=pltpu.CompilerParams(dimension_semantics=("arbitrary",),
                                             vmem_limit_bytes=V7X_VMEM_LIMIT),
        name="matmul",
    )(*operands)
    return out[:, :m] if m_pad != m else out


def matmul3(x, w, gain=None):
    b, s, k = x.shape
    return matmul(x.reshape(b * s, k), w, gain).reshape(b, s, w.shape[1])


def rmsnorm(x, g):
    xf = x.astype(F32)
    y = xf * lax.rsqrt(jnp.mean(xf * xf, axis=-1, keepdims=True) + EPS)
    return (y * g.astype(F32)).astype(x.dtype)


def rope(x, pos):
    half = x.shape[-1] // 2
    inv_freq = ROPE_THETA ** (-jnp.arange(half, dtype=F32) / half)
    ang = pos.astype(F32)[:, None] * inv_freq[None, :]
    cos = jnp.cos(ang)[None, :, None, :]
    sin = jnp.sin(ang)[None, :, None, :]
    xf = x.astype(F32)
    x1, x2 = xf[..., :half], xf[..., half:]
    return jnp.concatenate([x1 * cos - x2 * sin, x2 * cos + x1 * sin], axis=-1).astype(x.dtype)


ATTN_BLOCK = 512


def _flash_kernel(q_ref, k_ref, v_ref, fq_ref, fk_ref, o_ref, m_ref, l_ref, acc_ref, *, scale, use_forget):
    qi = pl.program_id(2)
    ki = pl.program_id(3)

    @pl.when(ki == 0)
    def _():
        m_ref[...] = jnp.full_like(m_ref, -jnp.inf)
        l_ref[...] = jnp.zeros_like(l_ref)
        acc_ref[...] = jnp.zeros_like(acc_ref)

    def step(diagonal):
        s = lax.dot_general(q_ref[0, 0], k_ref[0, 0], (((1,), (1,)), ((), ())),
                            preferred_element_type=F32) * scale
        if use_forget:
            s = s + fq_ref[0, 0] - fk_ref[0, 0]
        if diagonal:
            row = lax.broadcasted_iota(jnp.int32, s.shape, 0)
            col = lax.broadcasted_iota(jnp.int32, s.shape, 1)
            s = jnp.where(row >= col, s, -jnp.inf)
        m_prev = m_ref[...]
        m_new = jnp.maximum(m_prev, jnp.max(s, axis=-1, keepdims=True))
        p = jnp.exp(s - m_new)
        corr = jnp.exp(m_prev - m_new)
        l_ref[...] = corr * l_ref[...] + jnp.sum(p, axis=-1, keepdims=True)
        acc_ref[...] = corr * acc_ref[...] + jnp.dot(p.astype(BF16), v_ref[0, 0], preferred_element_type=F32)
        m_ref[...] = m_new

    @pl.when(ki < qi)
    def _():
        step(False)

    @pl.when(ki == qi)
    def _():
        step(True)
        o_ref[0, 0] = acc_ref[...] / l_ref[...]


def prompt_attention(q, k, v, fcum):
    bsz, seq, n_kv, n_grp, dk = q.shape
    dv = v.shape[-1]
    nh = n_kv * n_grp
    tb = math.gcd(seq, ATTN_BLOCK)
    nb = seq // tb
    qh = jnp.transpose(q.reshape(bsz, seq, nh, dk), (0, 2, 1, 3)).astype(BF16)
    kh = jnp.transpose(k, (0, 2, 1, 3)).astype(BF16)
    vh = jnp.transpose(v, (0, 2, 1, 3)).astype(BF16)
    use_forget = fcum is not None
    if use_forget:
        ft = jnp.transpose(fcum.reshape(bsz, seq, nh), (0, 2, 1)).astype(F32)
    else:
        ft = jnp.zeros((bsz, nh, seq), F32)
    fq = ft[..., None]
    fk = ft[:, :, None, :]
    out = pl.pallas_call(
        functools.partial(_flash_kernel, scale=dk ** -0.5, use_forget=use_forget),
        grid=(bsz, nh, nb, nb),
        in_specs=[pl.BlockSpec((1, 1, tb, dk), lambda b, h, i, j: (b, h, i, 0)),
                  pl.BlockSpec((1, 1, tb, dk), lambda b, h, i, j: (b, h // n_grp, jnp.minimum(i, j), 0)),
                  pl.BlockSpec((1, 1, tb, dv), lambda b, h, i, j: (b, h // n_grp, jnp.minimum(i, j), 0)),
                  pl.BlockSpec((1, 1, tb, 1), lambda b, h, i, j: (b, h, i, 0)),
                  pl.BlockSpec((1, 1, 1, tb), lambda b, h, i, j: (b, h, 0, jnp.minimum(i, j)))],
        out_specs=pl.BlockSpec((1, 1, tb, dv), lambda b, h, i, j: (b, h, i, 0)),
        out_shape=jax.ShapeDtypeStruct((bsz, nh, seq, dv), F32),
        scratch_shapes=[pltpu.VMEM((tb, 1), F32), pltpu.VMEM((tb, 1), F32), pltpu.VMEM((tb, dv), F32)],
        compiler_params=pltpu.CompilerParams(
            dimension_semantics=("arbitrary", "arbitrary", "arbitrary", "arbitrary"),
            vmem_limit_bytes=V7X_VMEM_LIMIT),
        name="flash_attention",
    )(qh, kh, vh, fq, fk)
    return jnp.transpose(out, (0, 2, 1, 3)).reshape(bsz, seq, n_kv, n_grp, dv)


DECODE_PAGES = 8
DECODE_ROWS = MLA_HEADS * 8


def _page_copies(pt_ref, layer, b, blk, slot, srcs, bufs, sems):
    out = []
    for p in range(DECODE_PAGES):
        page = pt_ref[b, blk * DECODE_PAGES + p]
        for src, buf, sem in zip(srcs, bufs, sems):
            out.append(pltpu.make_async_copy(src.at[layer, page], buf.at[slot, p], sem.at[slot]))
    return out


def _head_rows(x4):
    n = x4.shape[1]
    return jnp.concatenate([jnp.broadcast_to(x4[h:h + 1], (8, n)) for h in range(MLA_HEADS)], axis=0)


def _softmax_step(s, pv_fn, m_ref, l_ref, acc_ref):
    m_prev = m_ref[...]
    m_new = jnp.maximum(m_prev, jnp.max(s, axis=-1, keepdims=True))
    p = jnp.exp(s - m_new)
    corr = jnp.exp(m_prev - m_new)
    l_ref[...] = corr * l_ref[...] + jnp.sum(p, axis=-1, keepdims=True)
    acc_ref[...] = corr * acc_ref[...] + pv_fn(p.astype(BF16))
    m_ref[...] = m_new


def _paged_scores(q, pages):
    return jnp.concatenate([jnp.dot(q, pages[p], preferred_element_type=F32) for p in range(pages.shape[0])],
                           axis=1)


def _paged_values(w, pages):
    out = None
    for p in range(pages.shape[0]):
        part = lax.dot_general(w[:, p * PAGE_SIZE:(p + 1) * PAGE_SIZE], pages[p], (((1,), (1,)), ((), ())),
                               preferred_element_type=F32)
        out = part if out is None else out + part
    return out


def _decode_loop(pt_ref, layer, srcs, bufs, sems, block_fn, n_blocks, init):
    b = pl.program_id(0)
    for c in _page_copies(pt_ref, layer, b, n_blocks - 1, 0, srcs, bufs, sems):
        c.start()

    def body(it, carry):
        blk = n_blocks - 1 - it
        slot = jnp.bitwise_and(it, 1)

        @pl.when(it + 1 < n_blocks)
        def _():
            for c in _page_copies(pt_ref, layer, b, blk - 1, 1 - slot, srcs, bufs, sems):
                c.start()

        for c in _page_copies(pt_ref, layer, b, blk, slot, srcs, bufs, sems):
            c.wait()
        return block_fn(slot, carry)

    return lax.fori_loop(0, n_blocks, body, init)


def _fox_decode_kernel(pt_ref, q_ref, kn_ref, vn_ref, bn_ref, gq_ref, k_hbm, v_hbm, lf_hbm, o_ref,
                       kbuf, vbuf, lfbuf, sems, m_ref, l_ref, acc_ref, *, n_blocks, scale, layer):
    m_ref[...] = jnp.full_like(m_ref, -jnp.inf)
    l_ref[...] = jnp.zeros_like(l_ref)
    acc_ref[...] = jnp.zeros_like(acc_ref)
    q = q_ref[0]
    gq = gq_ref[0]
    lane = lax.broadcasted_iota(jnp.int32, (FOX_HEADS, V7X_LANES), 1)
    def block(slot, suf):
        biases = [None] * DECODE_PAGES
        for p in reversed(range(DECODE_PAGES)):
            x = lfbuf[slot, p]
            y = x
            for d in (1, 2, 4, 8, 16, 32, 64):
                y = y + jnp.where(lane < V7X_LANES - d, pltpu.roll(y, V7X_LANES - d, 1), 0.0)
            biases[p] = (y - x) + suf
            suf = suf + jnp.broadcast_to(y[:, 0:1], (FOX_HEADS, V7X_LANES))
        bias = jnp.concatenate([_head_rows(bp) for bp in biases], axis=1)
        vb = vbuf[slot].astype(BF16)
        s = _paged_scores(q, kbuf[slot].astype(BF16)) * scale
        _softmax_step(s + bias + gq, lambda w: _paged_values(w, vb), m_ref, l_ref, acc_ref)
        return suf

    _decode_loop(pt_ref, layer, (k_hbm, v_hbm, lf_hbm), (kbuf, vbuf, lfbuf), (sems.at[0], sems.at[1], sems.at[2]),
                 block, n_blocks, jnp.zeros((FOX_HEADS, V7X_LANES), F32))
    s = lax.dot_general(q, kn_ref[0], (((1,), (1,)), ((), ())), preferred_element_type=F32) * scale
    _softmax_step(s + bn_ref[0], lambda w: jnp.dot(w, vn_ref[0], preferred_element_type=F32), m_ref, l_ref, acc_ref)
    o_ref[0] = acc_ref[...] / l_ref[...]


def _pages_rows_last(cache):
    nl, npool, rows = cache.shape[:3]
    return jnp.moveaxis(cache, 2, -1).reshape(nl, npool, -1, rows)


def fox_decode(q, k_new, v_new, lf_new, cache_k, cache_v, cache_lft, page_table, layer):
    bd, t, n_kv, n_grp, dk = q.shape
    nh = n_kv * n_grp
    assert nh * t == DECODE_ROWS and n_kv * dk == V7X_LANES and t == 8
    n_pages = page_table.shape[1]
    n_blocks = n_pages // DECODE_PAGES
    qh = jnp.transpose(q.reshape(bd, t, nh, dk), (0, 2, 1, 3))
    zero = jnp.zeros((bd, n_grp * t, dk), q.dtype)
    qpad = jnp.concatenate(
        [jnp.concatenate([zero] * hk + [qh[:, hk * n_grp:(hk + 1) * n_grp].reshape(bd, n_grp * t, dk)]
                         + [zero] * (n_kv - 1 - hk), axis=-1) for hk in range(n_kv)], axis=1).astype(BF16)
    pad_rows = ((0, 0), (0, PAGE_SIZE - t), (0, 0))
    kn = jnp.pad(k_new.reshape(bd, t, n_kv * dk), pad_rows).astype(BF16)
    vn = jnp.pad(v_new.reshape(bd, t, n_kv * dk), pad_rows).astype(BF16)
    g = jnp.transpose(jnp.cumsum(lf_new.reshape(bd, t, nh).astype(F32), axis=1), (0, 2, 1))
    causal = jnp.tril(jnp.ones((t, t), bool))
    bn = jnp.where(causal, g[:, :, :, None] - g[:, :, None, :], -jnp.inf).reshape(bd, nh * t, t)
    bn = jnp.pad(bn, ((0, 0), (0, 0), (0, PAGE_SIZE - t)), constant_values=-jnp.inf)
    gq = g.reshape(bd, nh * t, 1)
    row_spec = lambda shape: pl.BlockSpec((1,) + shape, lambda b, pt: (b, 0, 0))
    out = pl.pallas_call(
        functools.partial(_fox_decode_kernel, n_blocks=n_blocks, scale=dk ** -0.5, layer=layer),
        grid_spec=pltpu.PrefetchScalarGridSpec(
            num_scalar_prefetch=1,
            grid=(bd,),
            in_specs=[row_spec((DECODE_ROWS, V7X_LANES)), row_spec((PAGE_SIZE, V7X_LANES)),
                      row_spec((PAGE_SIZE, V7X_LANES)), row_spec((DECODE_ROWS, PAGE_SIZE)),
                      row_spec((DECODE_ROWS, 1)),
                      pl.BlockSpec(memory_space=pl.ANY), pl.BlockSpec(memory_space=pl.ANY),
                      pl.BlockSpec(memory_space=pl.ANY)],
            out_specs=row_spec((DECODE_ROWS, V7X_LANES)),
            scratch_shapes=[pltpu.VMEM((2, DECODE_PAGES, PAGE_SIZE, V7X_LANES), F32),
                            pltpu.VMEM((2, DECODE_PAGES, PAGE_SIZE, V7X_LANES), F32),
                            pltpu.VMEM((2, DECODE_PAGES, nh, PAGE_SIZE), F32),
                            pltpu.SemaphoreType.DMA((3, 2)),
                            pltpu.VMEM((DECODE_ROWS, 1), F32), pltpu.VMEM((DECODE_ROWS, 1), F32),
                            pltpu.VMEM((DECODE_ROWS, V7X_LANES), F32)]),
        out_shape=jax.ShapeDtypeStruct((bd, DECODE_ROWS, V7X_LANES), F32),
        compiler_params=pltpu.CompilerParams(dimension_semantics=("arbitrary",),
                                             vmem_limit_bytes=V7X_VMEM_LIMIT),
        name="fox_decode",
    )(page_table, qpad, kn, vn, bn, gq, _pages_rows_last(cache_k), _pages_rows_last(cache_v), cache_lft)
    o = out.reshape(bd, n_kv, n_grp, t, n_kv, dk)
    o = jnp.stack([o[:, hk, :, :, hk] for hk in range(n_kv)], axis=1)
    return jnp.transpose(o, (0, 3, 1, 2, 4))


def _mla_decode_kernel(pt_ref, qn_ref, qr_ref, latn_ref, krn_ref, bn_ref, wuk_ref, seg_ref, wuv_ref,
                       lat_hbm, kr_hbm, o_ref, latbuf, krbuf, sems, m_ref, l_ref, acc_ref, *, n_blocks, scale, layer):
    m_ref[...] = jnp.full_like(m_ref, -jnp.inf)
    l_ref[...] = jnp.zeros_like(l_ref)
    acc_ref[...] = jnp.zeros_like(acc_ref)
    qn = qn_ref[0]
    qr = qr_ref[0]

    def scores(latb, s_rope):
        kn = jnp.dot(latb, wuk_ref[...], preferred_element_type=F32)
        ssq = lax.dot_general(seg_ref[...], (kn * kn).astype(BF16), (((1,), (1,)), ((), ())),
                              preferred_element_type=F32)
        inv = lax.rsqrt(ssq * (1.0 / MLA_NOPE) + EPS)
        s_nope = lax.dot_general(qn, kn.astype(BF16), (((1,), (1,)), ((), ())), preferred_element_type=F32)
        return (s_nope * _head_rows(inv) + s_rope) * scale

    rows = DECODE_PAGES * PAGE_SIZE

    def block(slot, carry):
        latb = latbuf[slot].reshape(rows, MLA_KV_RANK).astype(BF16)
        s_rope = _paged_scores(qr, krbuf[slot].astype(BF16))
        _softmax_step(scores(latb, s_rope), lambda w: jnp.dot(w, latb, preferred_element_type=F32),
                      m_ref, l_ref, acc_ref)
        return carry

    _decode_loop(pt_ref, layer, (lat_hbm, kr_hbm), (latbuf, krbuf), (sems.at[0], sems.at[1]), block, n_blocks,
                 jnp.zeros((8, V7X_LANES), F32))
    s_rope = lax.dot_general(qr, krn_ref[0], (((1,), (1,)), ((), ())), preferred_element_type=F32)
    _softmax_step(scores(latn_ref[0], s_rope) + bn_ref[0],
                  lambda w: jnp.dot(w, latn_ref[0], preferred_element_type=F32), m_ref, l_ref, acc_ref)
    ctx = (acc_ref[...] / l_ref[...]).astype(BF16)
    o_ref[0] = jnp.dot(ctx, wuv_ref[...], preferred_element_type=F32)


def mla_decode(q, lat_new, kr_new, cache_lat, cache_kr, page_table, layer, w_uk, w_uv, g_kn):
    bd, t, nh, _, _ = q.shape
    assert nh * t == DECODE_ROWS and t == 8
    n_pages = page_table.shape[1]
    n_blocks = n_pages // DECODE_PAGES
    qh = jnp.transpose(q[:, :, :, 0, :], (0, 2, 1, 3))
    qnope = qh[..., :MLA_NOPE] * g_kn.astype(F32)
    zero = jnp.zeros((bd, t, MLA_NOPE), F32)
    qn = jnp.concatenate([jnp.concatenate([zero] * h + [qnope[:, h]] + [zero] * (nh - 1 - h), axis=-1)
                          for h in range(nh)], axis=1).astype(BF16)
    qr = qh[..., MLA_NOPE:].reshape(bd, nh * t, MLA_ROPE).astype(BF16)
    pad_rows = ((0, 0), (0, PAGE_SIZE - t), (0, 0))
    latn = jnp.pad(lat_new, pad_rows).astype(BF16)
    krn = jnp.pad(kr_new, pad_rows).astype(BF16)
    causal = jnp.tril(jnp.ones((t, t), bool))
    bn = jnp.broadcast_to(jnp.where(causal, 0.0, -jnp.inf)[None, None], (bd, nh, t, t)).reshape(bd, nh * t, t)
    bn = jnp.pad(bn.astype(F32), ((0, 0), (0, 0), (0, PAGE_SIZE - t)), constant_values=-jnp.inf)
    wuk = w_uk.reshape(MLA_KV_RANK, nh * MLA_NOPE).astype(BF16)
    wuv = w_uv.reshape(MLA_KV_RANK, nh * MLA_V).astype(BF16)
    seg = (jnp.arange(8)[:, None] == (jnp.arange(nh * MLA_NOPE) // MLA_NOPE)[None, :]).astype(BF16)
    row_spec = lambda shape: pl.BlockSpec((1,) + shape, lambda b, pt: (b, 0, 0))
    full = lambda shape: pl.BlockSpec(shape, lambda b, pt: (0, 0))
    out = pl.pallas_call(
        functools.partial(_mla_decode_kernel, n_blocks=n_blocks, scale=(MLA_NOPE + MLA_ROPE) ** -0.5,
                          layer=layer),
        grid_spec=pltpu.PrefetchScalarGridSpec(
            num_scalar_prefetch=1,
            grid=(bd,),
            in_specs=[row_spec((DECODE_ROWS, nh * MLA_NOPE)), row_spec((DECODE_ROWS, MLA_ROPE)),
                      row_spec((PAGE_SIZE, MLA_KV_RANK)), row_spec((PAGE_SIZE, MLA_ROPE)),
                      row_spec((DECODE_ROWS, PAGE_SIZE)),
                      full((MLA_KV_RANK, nh * MLA_NOPE)), full((8, nh * MLA_NOPE)), full((MLA_KV_RANK, nh * MLA_V)),
                      pl.BlockSpec(memory_space=pl.ANY), pl.BlockSpec(memory_space=pl.ANY)],
            out_specs=row_spec((DECODE_ROWS, nh * MLA_V)),
            scratch_shapes=[pltpu.VMEM((2, DECODE_PAGES, PAGE_SIZE, MLA_KV_RANK), F32),
                            pltpu.VMEM((2, DECODE_PAGES, MLA_ROPE, PAGE_SIZE), F32),
                            pltpu.SemaphoreType.DMA((2, 2)),
                            pltpu.VMEM((DECODE_ROWS, 1), F32), pltpu.VMEM((DECODE_ROWS, 1), F32),
                            pltpu.VMEM((DECODE_ROWS, MLA_KV_RANK), F32)]),
        out_shape=jax.ShapeDtypeStruct((bd, DECODE_ROWS, nh * MLA_V), F32),
        compiler_params=pltpu.CompilerParams(dimension_semantics=("arbitrary",),
                                             vmem_limit_bytes=V7X_VMEM_LIMIT),
        name="mla_decode",
    )(page_table, qn, qr, latn, krn, bn, wuk, seg, wuv, cache_lat, _pages_rows_last(cache_kr))
    o = out.reshape(bd, nh, t, nh, MLA_V)
    o = jnp.stack([o[:, h, :, h] for h in range(nh)], axis=1)
    return jnp.transpose(o, (0, 2, 1, 3))[:, :, :, None, :]


def _online_update(m, l, acc, s, v):
    m_new = jnp.maximum(m, s.max(-1))
    p = jnp.exp(s - m_new[..., None])
    corr = jnp.exp(m - m_new)
    acc = acc * corr[..., None] + jnp.einsum('bhgts,bshe->bhgte', p, v.astype(F32))
    return m_new, l * corr + p.sum(-1), acc


def decode_attention(q, k_new, v_new, lf_new, load_past, n_blocks):
    bd, t, n_kv, n_grp, dk = q.shape
    dv = v_new.shape[-1]
    scale = dk ** -0.5
    use_forget = lf_new is not None
    if use_forget:
        g_new = jnp.transpose(jnp.cumsum(lf_new, axis=1), (0, 2, 3, 1))
        g_q = g_new[..., :, None]

    def past_block(carry, j):
        m, l, acc = carry[:3]
        k, v, lf = load_past(j)
        s = jnp.einsum('bthgd,bshd->bhgts', q, k).astype(F32) * scale
        extra = ()
        if use_forget:
            suf = carry[3]
            tot = lf.sum(axis=1)
            s_past = suf[:, None] + tot[:, None] - jnp.cumsum(lf, axis=1)
            s = s + jnp.transpose(s_past, (0, 2, 3, 1))[:, :, :, None, :] + g_q
            extra = (suf + tot,)
        m, l, acc = _online_update(m, l, acc, s, v)
        return (m, l, acc) + extra, None

    init = (jnp.full((bd, n_kv, n_grp, t), -jnp.inf, F32), jnp.zeros((bd, n_kv, n_grp, t), F32),
            jnp.zeros((bd, n_kv, n_grp, t, dv), F32))
    if use_forget:
        init = init + (jnp.zeros((bd, n_kv, n_grp), F32),)
    carry, _ = lax.scan(past_block, init, jnp.arange(n_blocks), reverse=True)
    m, l, acc = carry[:3]
    s = jnp.einsum('bthgd,bshd->bhgts', q, k_new).astype(F32) * scale
    if use_forget:
        s = s + g_q - g_new[..., None, :]
    causal = jnp.tril(jnp.ones((t, t), bool))
    s = jnp.where(causal, s, -jnp.inf)
    m, l, acc = _online_update(m, l, acc, s, v_new)
    out = acc / l[..., None]
    return jnp.transpose(out, (0, 3, 1, 2, 4)).astype(v_new.dtype)


def mlstm_chunkwise(q, k, v, ig, lf, c0, n0, m0):
    bsz, seq, nh, dk = q.shape
    dv = v.shape[-1]
    ln = math.gcd(seq, MLSTM_CHUNK)
    nc = seq // ln
    q = q.astype(F32)
    k = k.astype(F32) * dk ** -0.5
    v = v.astype(F32)

    def chunks(a):
        return jnp.moveaxis(a.reshape((bsz, nc, ln) + a.shape[2:]), 1, 0)

    causal = jnp.tril(jnp.ones((ln, ln), bool))[None, :, :, None]

    def step(carry, inp):
        c, n, m = carry
        qc, kc, vc, ic, fc = inp
        b = jnp.cumsum(fc, axis=1)
        dmat = b[:, :, None, :] - b[:, None, :, :] + ic[:, None, :, :]
        dmat = jnp.where(causal, dmat, -jnp.inf)
        inter = b + m[:, None, :]
        mt = jnp.maximum(inter, dmat.max(axis=2))
        w = jnp.exp(dmat - mt[:, :, None, :])
        ei = jnp.exp(inter - mt)
        a = w * jnp.einsum('bthd,bshd->btsh', qc, kc)
        num = ei[..., None] * jnp.einsum('bhed,bthd->bthe', c, qc) + jnp.einsum('btsh,bshe->bthe', a, vc)
        den = ei * jnp.einsum('bhd,bthd->bth', n, qc) + a.sum(axis=2)
        h = num / jnp.maximum(jnp.abs(den), jnp.exp(-mt))[..., None]
        b_last = b[:, -1]
        wlog = b_last[:, None] - b + ic
        m_new = jnp.maximum(b_last + m, wlog.max(axis=1))
        ws = jnp.exp(wlog - m_new[:, None])
        decay = jnp.exp(b_last + m - m_new)
        c_new = decay[..., None, None] * c + jnp.einsum('bsh,bshe,bshd->bhed', ws, vc, kc)
        n_new = decay[..., None] * n + jnp.einsum('bsh,bshd->bhd', ws, kc)
        return (c_new, n_new, m_new), h

    init = (c0.astype(F32), n0.astype(F32), m0.astype(F32))
    (c1, n1, m1), hs = lax.scan(step, init, (chunks(q), chunks(k), chunks(v), chunks(ig), chunks(lf)))
    h = jnp.moveaxis(hs, 0, 1).reshape(bsz, seq, nh, dv)
    return h, c1, n1, m1


def _gelu(x):
    return 0.5 * x * (1.0 + lax.erf(x * (2.0 ** -0.5)))


S5_NCH = S5_GROUPS * S5_STATE
S5_SCAN_ROWS = 8


def _s5_scan_block(xr, xi, pr, pi, cr, ci, row):
    for dist in (1, 2, 4):
        ar = pr[dist - 1:dist]
        ai = pi[dist - 1:dist]
        sr = jnp.where(row >= dist, pltpu.roll(xr, dist, 0), 0.0)
        si = jnp.where(row >= dist, pltpu.roll(xi, dist, 0), 0.0)
        xr, xi = xr + ar * sr - ai * si, xi + ar * si + ai * sr
    return xr + pr * cr - pi * ci, xi + pr * ci + pi * cr


def _s5_readout(u, xr_ref, xi_ref, cre_ref, cim_ref, d_ref, wglu_ref, bglu_ref):
    y = (jnp.dot(xr_ref[...].astype(BF16), cre_ref[...], preferred_element_type=F32)
         - jnp.dot(xi_ref[...].astype(BF16), cim_ref[...], preferred_element_type=F32)
         + u * d_ref[...])
    y = _gelu(y)
    gate = jnp.dot(y.astype(BF16), wglu_ref[...], preferred_element_type=F32) + bglu_ref[...]
    return y * jax.nn.sigmoid(gate)


def _s5_project(u, bre_ref, bim_ref, xr_ref, xi_ref):
    xr_ref[...] = jnp.dot(u, bre_ref[...], preferred_element_type=F32, precision=lax.Precision.HIGHEST)
    xi_ref[...] = jnp.dot(u, bim_ref[...], preferred_element_type=F32, precision=lax.Precision.HIGHEST)


def _s5_seq_kernel(u_ref, bre_ref, bim_ref, cre_ref, cim_ref, d_ref, wglu_ref, bglu_ref, p_ref, x0_ref,
                   o_ref, st_ref, xr_ref, xi_ref, car_ref):
    ts = u_ref.shape[1]
    u = u_ref[0]
    _s5_project(u, bre_ref, bim_ref, xr_ref, xi_ref)

    @pl.when(pl.program_id(1) == 0)
    def _():
        car_ref[...] = x0_ref[0]

    pr = p_ref[0]
    pi = p_ref[1]
    row = lax.broadcasted_iota(jnp.int32, (S5_SCAN_ROWS, S5_NCH), 0)

    def blk(j, carry):
        cr, ci = carry
        r0 = pl.multiple_of(j * S5_SCAN_ROWS, S5_SCAN_ROWS)
        xr, xi = _s5_scan_block(xr_ref[pl.ds(r0, S5_SCAN_ROWS), :], xi_ref[pl.ds(r0, S5_SCAN_ROWS), :],
                                pr, pi, cr, ci, row)
        xr_ref[pl.ds(r0, S5_SCAN_ROWS), :] = xr
        xi_ref[pl.ds(r0, S5_SCAN_ROWS), :] = xi
        return xr[S5_SCAN_ROWS - 1:], xi[S5_SCAN_ROWS - 1:]

    cr, ci = lax.fori_loop(0, ts // S5_SCAN_ROWS, blk, (car_ref[0:1], car_ref[1:2]))
    car_ref[0:1] = cr
    car_ref[1:2] = ci
    st_ref[0] = car_ref[...]
    o_ref[0] = _s5_readout(u, xr_ref, xi_ref, cre_ref, cim_ref, d_ref, wglu_ref, bglu_ref)


def _s5_short_kernel(u_ref, bre_ref, bim_ref, cre_ref, cim_ref, d_ref, wglu_ref, bglu_ref, p_ref, x0_ref,
                     o_ref, st_ref, xr_ref, xi_ref):
    nseq = x0_ref.shape[0]
    u = u_ref[...]
    _s5_project(u, bre_ref, bim_ref, xr_ref, xi_ref)
    pr = p_ref[0]
    pi = p_ref[1]
    row = lax.broadcasted_iota(jnp.int32, (S5_SCAN_ROWS, S5_NCH), 0)

    def blk(j, carry):
        r0 = pl.multiple_of(j * S5_SCAN_ROWS, S5_SCAN_ROWS)
        x0 = x0_ref[j]
        xr, xi = _s5_scan_block(xr_ref[pl.ds(r0, S5_SCAN_ROWS), :], xi_ref[pl.ds(r0, S5_SCAN_ROWS), :],
                                pr, pi, x0[0:1], x0[1:2], row)
        xr_ref[pl.ds(r0, S5_SCAN_ROWS), :] = xr
        xi_ref[pl.ds(r0, S5_SCAN_ROWS), :] = xi
        st_ref[j, 0:1, :] = xr[S5_SCAN_ROWS - 1:]
        st_ref[j, 1:2, :] = xi[S5_SCAN_ROWS - 1:]
        return carry

    lax.fori_loop(0, nseq, blk, 0)
    o_ref[...] = _s5_readout(u, xr_ref, xi_ref, cre_ref, cim_ref, d_ref, wglu_ref, bglu_ref)


def _s5_tables(a_re, a_im, log_step, b_re, b_im, c_re, c_im):
    step = jnp.exp(log_step.astype(F32))[:, None]
    mag = jnp.exp(a_re * step)
    abr = mag * jnp.cos(a_im * step)
    abi = mag * jnp.sin(a_im * step)
    den = a_re * a_re + a_im * a_im
    cr = ((abr - 1.0) * a_re + abi * a_im) / den
    ci = (abi * a_re - (abr - 1.0) * a_im) / den
    bbr = cr[..., None] * b_re - ci[..., None] * b_im
    bbi = cr[..., None] * b_im + ci[..., None] * b_re
    eye = jnp.eye(S5_GROUPS, dtype=F32)
    bre = jnp.einsum('gpi,gh->gihp', bbr, eye).reshape(S5_CH, S5_NCH)
    bim = jnp.einsum('gpi,gh->gihp', bbi, eye).reshape(S5_CH, S5_NCH)
    cre = jnp.einsum('gop,gh->gpho', c_re, eye).reshape(S5_NCH, S5_CH).astype(BF16)
    cim = jnp.einsum('gop,gh->gpho', c_im, eye).reshape(S5_NCH, S5_CH).astype(BF16)
    ar = abr.reshape(1, S5_NCH)
    ai = abi.reshape(1, S5_NCH)
    pws = [(ar, ai)]
    for _ in range(S5_SCAN_ROWS - 1):
        qr, qi = pws[-1]
        pws.append((qr * ar - qi * ai, qr * ai + qi * ar))
    powers = jnp.stack([jnp.concatenate([p[0] for p in pws], axis=0),
                        jnp.concatenate([p[1] for p in pws], axis=0)], axis=0)
    return bre, bim, cre, cim, powers


def s5_layer(u, a_re, a_im, log_step, b_re, b_im, c_re, c_im, d, w_glu, b_glu, x0_re, x0_im):
    bsz, seq, _ = u.shape
    bre, bim, cre, cim, powers = _s5_tables(a_re, a_im, log_step, b_re, b_im, c_re, c_im)
    x0 = jnp.stack([x0_re.reshape(bsz, S5_NCH), x0_im.reshape(bsz, S5_NCH)], axis=1)
    d2 = d.astype(F32).reshape(1, S5_CH)
    bg = b_glu.astype(F32).reshape(1, S5_CH)
    wg = w_glu.astype(BF16)
    const = lambda *_: (0, 0)
    weight_specs = [pl.BlockSpec((S5_CH, S5_NCH), const), pl.BlockSpec((S5_CH, S5_NCH), const),
                    pl.BlockSpec((S5_NCH, S5_CH), const), pl.BlockSpec((S5_NCH, S5_CH), const),
                    pl.BlockSpec((1, S5_CH), const), pl.BlockSpec((S5_CH, S5_CH), const),
                    pl.BlockSpec((1, S5_CH), const),
                    pl.BlockSpec((2, S5_SCAN_ROWS, S5_NCH), lambda *_: (0, 0, 0))]
    weights = (bre, bim, cre, cim, d2, wg, bg, powers)
    if seq == S5_SCAN_ROWS:
        nseq = math.gcd(bsz, 64)
        rows = nseq * seq
        out, st = pl.pallas_call(
            _s5_short_kernel,
            grid=(bsz // nseq,),
            in_specs=[pl.BlockSpec((rows, S5_CH), lambda i: (i, 0))] + weight_specs
                     + [pl.BlockSpec((nseq, 2, S5_NCH), lambda i: (i, 0, 0))],
            out_specs=[pl.BlockSpec((rows, S5_CH), lambda i: (i, 0)),
                       pl.BlockSpec((nseq, 2, S5_NCH), lambda i: (i, 0, 0))],
            out_shape=[jax.ShapeDtypeStruct((bsz * seq, S5_CH), F32),
                       jax.ShapeDtypeStruct((bsz, 2, S5_NCH), F32)],
            scratch_shapes=[pltpu.VMEM((rows, S5_NCH), F32), pltpu.VMEM((rows, S5_NCH), F32)],
            compiler_params=pltpu.CompilerParams(dimension_semantics=("arbitrary",),
                                                 vmem_limit_bytes=V7X_VMEM_LIMIT),
            name="s5_short",
        )(u.reshape(bsz * seq, S5_CH), *weights, x0)
        out = out.reshape(bsz, seq, S5_CH)
    else:
        ts = math.gcd(seq, 512)
        out, st = pl.pallas_call(
            _s5_seq_kernel,
            grid=(bsz, seq // ts),
            in_specs=[pl.BlockSpec((1, ts, S5_CH), lambda b, s: (b, s, 0))] + weight_specs
                     + [pl.BlockSpec((1, 2, S5_NCH), lambda b, s: (b, 0, 0))],
            out_specs=[pl.BlockSpec((1, ts, S5_CH), lambda b, s: (b, s, 0)),
                       pl.BlockSpec((1, 2, S5_NCH), lambda b, s: (b, 0, 0))],
            out_shape=[jax.ShapeDtypeStruct((bsz, seq, S5_CH), F32),
                       jax.ShapeDtypeStruct((bsz, 2, S5_NCH), F32)],
            scratch_shapes=[pltpu.VMEM((ts, S5_NCH), F32), pltpu.VMEM((ts, S5_NCH), F32),
                            pltpu.VMEM((2, S5_NCH), F32)],
            compiler_params=pltpu.CompilerParams(dimension_semantics=("arbitrary", "arbitrary"),
                                                 vmem_limit_bytes=V7X_VMEM_LIMIT),
            name="s5_seq",
        )(u, *weights, x0)
    s_re = st[:, 0].reshape(bsz, S5_GROUPS, S5_STATE)
    s_im = st[:, 1].reshape(bsz, S5_GROUPS, S5_STATE)
    return out, s_re, s_im


def mla_keys(lat, krope, w_uk, w_uv, g_kn):
    k_nope = rmsnorm(jnp.einsum('bsr,rhd->bshd', lat, w_uk), g_kn)
    v = jnp.einsum('bsr,rhd->bshd', lat, w_uv)
    kr = jnp.broadcast_to(krope[:, :, None, :], k_nope.shape[:3] + (MLA_ROPE,)).astype(k_nope.dtype)
    return jnp.concatenate([k_nope, kr], axis=-1), v


PEER_HALF = PEER_KEY_DIM // 2
PEER_QW = PEER_HEADS * PEER_KEY_DIM
PEER_ROUTE_TOKENS = 256
PEER_TOKENS = 512
PEER_KEY_ROWS = 8
PEER_EXPERT_BLOCK = PEER_KEY_ROWS * PEER_NKEYS
PEER_ROW_BLOCK = 128


def _extract_top(xs, rows_f):
    slot = lax.broadcasted_iota(jnp.int32, (PEER_TOPK, V7X_LANES), 0)
    nrows = float(xs[0].shape[0])

    def body(r, carry):
        out = []
        for x, sv in carry:
            m = jnp.max(x, axis=0, keepdims=True)
            first = jnp.min(jnp.where(x == m, rows_f, nrows), axis=0, keepdims=True)
            out.append((jnp.where(rows_f == first, -jnp.inf, x), jnp.where(slot == r, m, sv)))
        return tuple(out)

    init = tuple((x, jnp.zeros((PEER_TOPK, V7X_LANES), F32)) for x in xs)
    return [sv for _, sv in lax.fori_loop(0, PEER_TOPK, body, init)]


PEER_PAIR_PIECES = tuple((b, PEER_TOPK // (b + 1)) for b in range(1, 8))
PEER_PAIR_ROWS = PEER_TOPK + 8 * len(PEER_PAIR_PIECES) + 8


def _pair_candidates(sv0, sv1):
    row8 = lax.broadcasted_iota(jnp.int32, (8, V7X_LANES), 0)
    pieces = [sv0 + sv1[0:1]]
    for b, n_valid in PEER_PAIR_PIECES:
        pieces.append(jnp.where(row8 < n_valid, sv0[0:8] + sv1[b:b + 1], -jnp.inf))
    pieces.append(sv0[0:1] + sv1[8:16])
    return jnp.concatenate(pieces, axis=0)


def _peer_route_kernel(h_ref, g_ref, wqt_ref, sk_ref, xb_ref, s0_ref, scl_ref, s1_ref, e1_ref, tau_ref,
                       st_ref):
    tr = h_ref.shape[0]
    hf = h_ref[...]
    c = hf * lax.rsqrt(jnp.mean(hf * hf, axis=-1, keepdims=True) + EPS) * g_ref[...]
    cb = c.astype(BF16)
    xb_ref[...] = cb
    qt = lax.dot_general(wqt_ref[...], cb, (((1,), (1,)), ((), ())), preferred_element_type=F32)
    for hc in range(2 * PEER_HEADS):
        st_ref[hc] = jnp.dot(sk_ref[hc], qt[hc * PEER_HALF:(hc + 1) * PEER_HALF].astype(BF16),
                             preferred_element_type=F32)

    rows_key = lax.broadcasted_iota(jnp.int32, (PEER_NKEYS, V7X_LANES), 0).astype(F32)
    rows_pair = lax.broadcasted_iota(jnp.int32, (PEER_PAIR_ROWS, V7X_LANES), 0).astype(F32)
    lane_groups = [slice(lg * V7X_LANES, (lg + 1) * V7X_LANES) for lg in range(tr // V7X_LANES)]

    def head(h, carry):
        tops = []
        for lanes in lane_groups:
            s0 = st_ref[2 * h, :, lanes]
            s1 = st_ref[2 * h + 1, :, lanes]
            tops.append(_extract_top([s0, s1], rows_key))
        bests = _extract_top([_pair_candidates(sv0, sv1) for sv0, sv1 in tops], rows_pair)
        for lanes, (sv0, sv1), best in zip(lane_groups, tops, bests):
            s0 = st_ref[2 * h, :, lanes]
            s1 = st_ref[2 * h + 1, :, lanes]
            z = jnp.sum(jnp.exp(best - best[0:1]), axis=0, keepdims=True)
            s0_ref[h, :, lanes] = s0
            scl_ref[h, :, lanes] = jnp.exp(s0 - sv0[0:1]) / z
            s1_ref[h, :, lanes] = s1
            e1_ref[h, :, lanes] = jnp.exp(s1 - sv1[0:1])
            tau_ref[h, :, lanes] = best[PEER_TOPK - 1:PEER_TOPK]
        return carry

    lax.fori_loop(0, PEER_HEADS, head, 0)


def _peer_expert_kernel(x_ref, u_ref, vt_ref, s0_ref, scl_ref, s1_ref, e1_ref, tau_ref, o_ref,
                        acc_ref, a_ref, s_ref):
    ei = pl.program_id(1)
    tt = x_ref.shape[0]
    key_rows = u_ref.shape[0] // PEER_NKEYS

    @pl.when(ei == 0)
    def _():
        acc_ref[...] = jnp.zeros_like(acc_ref)

    s_ref[...] = lax.dot_general(u_ref[...], x_ref[...], (((1,), (1,)), ((), ())), preferred_element_type=F32)

    assert key_rows == PEER_KEY_ROWS
    first_keys = pl.ds(pl.multiple_of(ei * PEER_KEY_ROWS, PEER_KEY_ROWS), PEER_KEY_ROWS)
    for lg in range(tt // V7X_LANES):
        lanes = slice(lg * V7X_LANES, (lg + 1) * V7X_LANES)
        s0rows = [s0_ref[h, first_keys, lanes] for h in range(PEER_HEADS)]
        sclrows = [scl_ref[h, first_keys, lanes] for h in range(PEER_HEADS)]
        taurow = [tau_ref[h, :, lanes] for h in range(PEER_HEADS)]
        for r in range(key_rows):
            s0row = [v[r:r + 1] for v in s0rows]
            sclrow = [v[r:r + 1] for v in sclrows]
            for jb in range(PEER_NKEYS // PEER_ROW_BLOCK):
                rs = slice(jb * PEER_ROW_BLOCK, (jb + 1) * PEER_ROW_BLOCK)
                es = slice(r * PEER_NKEYS + jb * PEER_ROW_BLOCK, r * PEER_NKEYS + (jb + 1) * PEER_ROW_BLOCK)
                gate = jnp.zeros((PEER_ROW_BLOCK, V7X_LANES), F32)
                for h in range(PEER_HEADS):
                    pair = s1_ref[h, rs, lanes] + s0row[h]
                    gate = gate + jnp.where(pair >= taurow[h], e1_ref[h, rs, lanes], 0.0) * sclrow[h]
                a_ref[es, lanes] = (gate * _gelu(s_ref[es, lanes])).astype(BF16)

    acc_ref[...] += jnp.dot(vt_ref[...], a_ref[...], preferred_element_type=F32)

    @pl.when(ei == pl.num_programs(1) - 1)
    def _():
        o_ref[...] = acc_ref[...].T


def _peer_tables_kernel(u_ref, v_ref, ub_ref, vt_ref):
    ub_ref[...] = u_ref[...].astype(BF16)
    vt_ref[...] = v_ref[...].T.astype(BF16)


def peer_tables(u_tab, v_tab):
    nexp, dm = u_tab.shape
    te = math.gcd(nexp, 512)
    return pl.pallas_call(
        _peer_tables_kernel,
        grid=(nexp // te,),
        in_specs=[pl.BlockSpec((te, dm), lambda i: (i, 0)), pl.BlockSpec((te, dm), lambda i: (i, 0))],
        out_specs=[pl.BlockSpec((te, dm), lambda i: (i, 0)), pl.BlockSpec((dm, te), lambda i: (0, i))],
        out_shape=[jax.ShapeDtypeStruct((nexp, dm), BF16), jax.ShapeDtypeStruct((dm, nexp), BF16)],
        compiler_params=pltpu.CompilerParams(dimension_semantics=("arbitrary",),
                                             vmem_limit_bytes=V7X_VMEM_LIMIT),
        name="peer_tables",
    )(u_tab, v_tab)


def peer_ffn(hres, g_ffn, w_q, subkeys, ub, vt):
    bsz, seq, dm = hres.shape
    ntok = bsz * seq
    nexp = ub.shape[0]
    tr = math.gcd(ntok, PEER_ROUTE_TOKENS)
    tt = math.gcd(ntok, PEER_TOKENS)
    wqt = w_q.T.astype(BF16)
    sk = subkeys.reshape(2 * PEER_HEADS, PEER_NKEYS, PEER_HALF).astype(BF16)
    key_shape = (PEER_HEADS, PEER_NKEYS, ntok)
    key_block = lambda t: pl.BlockSpec((PEER_HEADS, PEER_NKEYS, t), lambda i, *_: (0, 0, i))
    tau_block = lambda t: pl.BlockSpec((PEER_HEADS, 1, t), lambda i, *_: (0, 0, i))
    xb, s0t, sclt, s1t, e1t, taut = pl.pallas_call(
        _peer_route_kernel,
        grid=(ntok // tr,),
        in_specs=[pl.BlockSpec((tr, dm), lambda i: (i, 0)),
                  pl.BlockSpec((1, dm), lambda i: (0, 0)),
                  pl.BlockSpec((PEER_QW, dm), lambda i: (0, 0)),
                  pl.BlockSpec((2 * PEER_HEADS, PEER_NKEYS, PEER_HALF), lambda i: (0, 0, 0))],
        out_specs=[pl.BlockSpec((tr, dm), lambda i: (i, 0)),
                   key_block(tr), key_block(tr), key_block(tr), key_block(tr), tau_block(tr)],
        out_shape=[jax.ShapeDtypeStruct((ntok, dm), BF16)] + [jax.ShapeDtypeStruct(key_shape, F32)] * 4
                  + [jax.ShapeDtypeStruct((PEER_HEADS, 1, ntok), F32)],
        scratch_shapes=[pltpu.VMEM((2 * PEER_HEADS, PEER_NKEYS, tr), F32)],
        compiler_params=pltpu.CompilerParams(dimension_semantics=("arbitrary",),
                                             vmem_limit_bytes=V7X_VMEM_LIMIT),
        name="peer_route",
    )(hres.reshape(ntok, dm), g_ffn.astype(F32).reshape(1, dm), wqt, sk)

    eb = PEER_EXPERT_BLOCK
    out = pl.pallas_call(
        _peer_expert_kernel,
        grid=(ntok // tt, nexp // eb),
        in_specs=[pl.BlockSpec((tt, dm), lambda t, e: (t, 0)),
                  pl.BlockSpec((eb, dm), lambda t, e: (e, 0)),
                  pl.BlockSpec((dm, eb), lambda t, e: (0, e)),
                  key_block(tt), key_block(tt), key_block(tt), key_block(tt), tau_block(tt)],
        out_specs=pl.BlockSpec((tt, dm), lambda t, e: (t, 0)),
        out_shape=jax.ShapeDtypeStruct((ntok, dm), F32),
        scratch_shapes=[pltpu.VMEM((dm, tt), F32), pltpu.VMEM((eb, tt), BF16), pltpu.VMEM((eb, tt), F32)],
        compiler_params=pltpu.CompilerParams(dimension_semantics=("arbitrary", "arbitrary"),
                                             vmem_limit_bytes=V7X_VMEM_LIMIT),
        name="peer_experts",
    )(xb, ub, vt, s0t, sclt, s1t, e1t, taut)
    return out.reshape(bsz, seq, dm)


def mixing_sublayer(a, pos, lp, past):
    bsz, seq, _ = a.shape
    z = matmul3(a, lp['w_in'], lp['g_mix'])
    points = [int(t) for t in np.cumsum(SPLIT_SIZES)[:-1]]
    (fq, fk, fv, ff, mq, mk, mv, mi, mf, mo, su, cq, ckv, kr) = jnp.split(z, points, axis=-1)

    q_f = rmsnorm(fq.reshape(bsz, seq, FOX_KV_HEADS, FOX_GROUP, HEAD_DIM), lp['g_fox_q'])
    k_f = rmsnorm(fk.reshape(bsz, seq, FOX_KV_HEADS, HEAD_DIM), lp['g_fox_k'])
    v_f = fv.reshape(bsz, seq, FOX_KV_HEADS, HEAD_DIM)
    lf_f = jax.nn.log_sigmoid(ff.astype(F32) + lp['b_fox_f'].astype(F32))
    lf_g = lf_f.reshape(bsz, seq, FOX_KV_HEADS, FOX_GROUP)

    c_q = rmsnorm(cq, lp['g_mla_cq'])
    q_m = jnp.einsum('bsr,rhd->bshd', c_q, lp['w_mla_uq'])
    q_m = jnp.concatenate([rmsnorm(q_m[..., :MLA_NOPE], lp['g_mla_qn']),
                           rope(rmsnorm(q_m[..., MLA_NOPE:], lp['g_mla_qr']), pos)], axis=-1)[:, :, :, None, :]
    lat = rmsnorm(ckv, lp['g_mla_ckv'])
    krope = rope(rmsnorm(kr, lp['g_mla_kr'])[:, :, None, :], pos)[:, :, 0, :]
    k_m, v_m = mla_keys(lat, krope, lp['w_mla_uk'], lp['w_mla_uv'], lp['g_mla_kn'])

    if past is None:
        o_f = prompt_attention(q_f, k_f, v_f, jnp.cumsum(lf_g, axis=1))
        o_m = prompt_attention(q_m, k_m, v_m, None)
        c0 = jnp.zeros((bsz, MLSTM_HEADS, HEAD_DIM, HEAD_DIM), F32)
        n0 = jnp.zeros((bsz, MLSTM_HEADS, HEAD_DIM), F32)
        m0 = jnp.zeros((bsz, MLSTM_HEADS), F32)
        x0_re = jnp.zeros((bsz, S5_GROUPS, S5_STATE), F32)
        x0_im = jnp.zeros((bsz, S5_GROUPS, S5_STATE), F32)
    else:
        o_f = fox_decode(q_f, k_f, v_f, lf_g, past['fox_k'], past['fox_v'], past['fox_lft'], past['page_table'],
                         past['layer'])
        o_m = mla_decode(q_m, lat, krope, past['mla_lat'], past['mla_kr'], past['page_table'], past['layer'],
                         lp['w_mla_uk'], lp['w_mla_uv'], lp['g_mla_kn'])
        c0, n0, m0, x0_re, x0_im = past['rec']

    ig = mi.astype(F32) + lp['b_mlstm_i'].astype(F32)
    lf_l = jax.nn.log_sigmoid(mf.astype(F32) + lp['b_mlstm_f'].astype(F32))
    h_l, c1, n1, m1 = mlstm_chunkwise(mq.reshape(bsz, seq, MLSTM_HEADS, HEAD_DIM),
                                      mk.reshape(bsz, seq, MLSTM_HEADS, HEAD_DIM),
                                      mv.reshape(bsz, seq, MLSTM_HEADS, HEAD_DIM), ig, lf_l, c0, n0, m0)
    o_l = rmsnorm(h_l, lp['g_mlstm_h']).astype(a.dtype).reshape(bsz, seq, MLSTM_W) * jax.nn.sigmoid(mo)

    o_s, s_re, s_im = s5_layer(su, lp['s5_a_re'], lp['s5_a_im'], lp['s5_log_step'], lp['s5_b_re'], lp['s5_b_im'],
                               lp['s5_c_re'], lp['s5_c_im'], lp['s5_d'], lp['w_glu'], lp['b_glu'], x0_re, x0_im)

    mix = jnp.concatenate([o_f.reshape(bsz, seq, -1), o_l, o_s, o_m.reshape(bsz, seq, -1)], axis=-1)
    out = matmul3(mix, lp['w_out'])
    return out, (k_f, v_f, lf_f, lat, krope, c1, n1, m1, s_re, s_im)


def run_trunk(x, p, pos, layer_params, pasts):
    h = x
    states = []
    for i in range(DEPTH):
        lp = layer_params[i]
        mix, st = mixing_sublayer(h, pos, lp, pasts[i])
        h = h + mix
        h = h + peer_ffn(h, lp['g_ffn'], lp['w_peer_q'], lp['peer_subkeys'], lp['peer_ub'], lp['peer_vt'])
        gate = jax.nn.sigmoid(matmul3(h, lp['w_ple_gate'], lp['g_ple']))
        h = h + matmul3(p[i], lp['w_ple']) * gate
        states.append(st)
    stacked = [jnp.stack([st[j] for st in states], axis=0) for j in range(len(states[0]))]
    return h, stacked


def kernel(x_prompt, x_sample, cache_fox_k, cache_fox_v, cache_fox_logf, cache_mla_latent, cache_mla_krope,
           state_mlstm_C, state_mlstm_n, state_mlstm_m, state_s5_re, state_s5_im, page_table, p_prompt, p_sample,
           g_mix, w_in, g_fox_q, g_fox_k, b_fox_f, b_mlstm_i, b_mlstm_f, g_mlstm_h,
           s5_a_re, s5_a_im, s5_log_step, s5_b_re, s5_b_im, s5_c_re, s5_c_im, s5_d, w_glu, b_glu,
           g_mla_cq, w_mla_uq, g_mla_qn, g_mla_qr, g_mla_ckv, g_mla_kr, w_mla_uk, w_mla_uv, g_mla_kn,
           w_out, g_ffn, w_peer_q, peer_subkeys, peer_u, peer_v, g_ple, w_ple_gate, w_ple):
    names = ('g_mix', 'w_in', 'g_fox_q', 'g_fox_k', 'b_fox_f', 'b_mlstm_i', 'b_mlstm_f', 'g_mlstm_h',
             's5_a_re', 's5_a_im', 's5_log_step', 's5_b_re', 's5_b_im', 's5_c_re', 's5_c_im', 's5_d', 'w_glu',
             'b_glu', 'g_mla_cq', 'w_mla_uq', 'g_mla_qn', 'g_mla_qr', 'g_mla_ckv', 'g_mla_kr', 'w_mla_uk',
             'w_mla_uv', 'g_mla_kn', 'w_out', 'g_ffn', 'w_peer_q', 'peer_subkeys', 'peer_u', 'peer_v', 'g_ple',
             'w_ple_gate', 'w_ple')
    vals = (g_mix, w_in, g_fox_q, g_fox_k, b_fox_f, b_mlstm_i, b_mlstm_f, g_mlstm_h,
            s5_a_re, s5_a_im, s5_log_step, s5_b_re, s5_b_im, s5_c_re, s5_c_im, s5_d, w_glu,
            b_glu, g_mla_cq, w_mla_uq, g_mla_qn, g_mla_qr, g_mla_ckv, g_mla_kr, w_mla_uk,
            w_mla_uv, g_mla_kn, w_out, g_ffn, w_peer_q, peer_subkeys, peer_u, peer_v, g_ple,
            w_ple_gate, w_ple)
    layer_params = [{n: v[i] for n, v in zip(names, vals)} for i in range(DEPTH)]
    for lp in layer_params:
        lp['peer_ub'], lp['peer_vt'] = peer_tables(lp['peer_u'], lp['peer_v'])

    past_len = page_table.shape[1] * PAGE_SIZE
    cache_lft = jnp.swapaxes(cache_fox_logf, 2, 3)

    def make_past(i):
        return {'fox_k': cache_fox_k, 'fox_v': cache_fox_v, 'fox_lft': cache_lft, 'layer': i,
                'mla_lat': cache_mla_latent, 'mla_kr': cache_mla_krope, 'page_table': page_table,
                'rec': (state_mlstm_C[i], state_mlstm_n[i], state_mlstm_m[i], state_s5_re[i], state_s5_im[i])}

    pasts = [make_past(i) for i in range(DEPTH)]
    pos_s = past_len + jnp.arange(x_sample.shape[1], dtype=jnp.int32)
    y_sample, st_s = run_trunk(x_sample, p_sample, pos_s, layer_params, pasts)

    pos_p = jnp.arange(x_prompt.shape[1], dtype=jnp.int32)
    y_prompt, st_p = run_trunk(x_prompt, p_prompt, pos_p, layer_params, [None] * DEPTH)

    (pk, pv, plf, plat, pkr, pc, pn, pm, pre, pim) = st_p
    (sk, sv, slf, slat, skr, sc, sn, sm, sre, sim) = st_s
    return (y_prompt, y_sample, pk, pv, plf, plat, pkr, pc, pn, pm, pre, pim,
            sk, sv, slf, slat, skr, sc, sn, sm, sre, sim)
```
